```python
import math
import jax
import jax.numpy as jnp
from jax import lax
import numpy as np

D_MODEL = 1024
BATCH = 4
SEQ = 8192
DEPTH = 4

HEAD_DIM = 64
NSA_HEADS = 8
NSA_GROUPS = 2
CMP_LEN = 32
CMP_STRIDE = 16
CMP_HIDDEN = 256
SEL_BLOCK = 64
SEL_TOP_N = 16
NSA_WINDOW = 512
NSA_Q_BLOCK = 64
SWA_HEADS = 8
SWA_KV_HEADS = 2
SWA_WINDOW = 128
SWA_Q_BLOCK = 128
REL_BUCKETS = 32
REL_MAX_DIST = 128
N_HEADS_TOTAL = NSA_HEADS + SWA_HEADS
N_EXPERTS = 32
TOP_K = 4
D_EXPERT = 1024
SWIGLU_ALPHA = 1.702
SWIGLU_LIMIT = 7.0
MOE_BLOCK = 256
RMS_EPS = 1e-5

NSA_Q_W = NSA_HEADS * HEAD_DIM
NSA_KV_W = NSA_GROUPS * HEAD_DIM
SWA_Q_W = SWA_HEADS * HEAD_DIM
SWA_KV_W = SWA_KV_HEADS * HEAD_DIM
IN_SPLITS = (NSA_Q_W, 6 * NSA_KV_W, 3 * NSA_HEADS, SWA_Q_W, 2 * SWA_KV_W, 2 * D_MODEL)
D_IN = NSA_Q_W + 6 * NSA_KV_W + 3 * NSA_HEADS + SWA_Q_W + 2 * SWA_KV_W + 2 * D_MODEL

kernel_name = 'hybrid_nsa_swa_sink_moe_adaln'


def rms_norm(x, g):
    xf = x.astype(jnp.float32)
    y = xf * lax.rsqrt(jnp.mean(xf * xf, axis=-1, keepdims=True) + RMS_EPS)
    return (y * g.astype(jnp.float32)).astype(x.dtype)


def masked_softmax(s, mask):
    s = jnp.where(mask, s.astype(jnp.float32), -jnp.inf)
    m = jnp.max(s, axis=-1, keepdims=True)
    m = jnp.where(jnp.isfinite(m), m, 0.0)
    e = jnp.exp(s - m)
    d = jnp.sum(e, axis=-1, keepdims=True)
    return e / jnp.where(d > 0, d, 1.0)


def t5_bucket(dist):
    n = jnp.maximum(dist, 0)
    max_exact = REL_BUCKETS // 2
    log_ratio = jnp.log(jnp.maximum(n, 1).astype(jnp.float32) / max_exact) / math.log(REL_MAX_DIST / max_exact)
    large = max_exact + (log_ratio * (REL_BUCKETS - max_exact)).astype(jnp.int32)
    return jnp.where(n < max_exact, n, jnp.minimum(large, REL_BUCKETS - 1))


def compress(kv, pos, w1, w2):
    B, S, G, dh = kv.shape
    c = CMP_LEN // CMP_STRIDE
    n_chunks = S // CMP_STRIDE
    n_cmp = n_chunks - c + 1
    ch = kv.reshape(B, n_chunks, CMP_STRIDE, G, dh)
    blocks = jnp.concatenate([ch[:, r:r + n_cmp] for r in range(c)], axis=2)
    blocks = blocks + pos[:, None, :]
    flat = jnp.swapaxes(blocks, 2, 3).reshape(B, n_cmp, G, CMP_LEN * dh)
    return jax.nn.gelu(flat @ w1) @ w2


def nsa_attention(q, k_cmp, v_cmp, k_slc, v_slc, k_win, v_win, branch_gate, rel_tab):
    B, S, G, Hg, dh = q.shape
    n_cmp = k_cmp.shape[1]
    n_blk = S // SEL_BLOCK
    n_sel = min(SEL_TOP_N, n_blk)
    r = SEL_BLOCK // CMP_STRIDE
    c = CMP_LEN // CMP_STRIDE
    pad_front = r + c - 2
    scale = HEAD_DIM ** -0.5
    cmp_end = jnp.arange(n_cmp) * CMP_STRIDE + CMP_LEN - 1
    kb = k_slc.reshape(B, n_blk, SEL_BLOCK, G, dh).transpose(0, 3, 1, 2, 4)
    vb = v_slc.reshape(B, n_blk, SEL_BLOCK, G, dh).transpose(0, 3, 1, 2, 4)
    kw = jnp.pad(k_win, ((0, 0), (NSA_WINDOW, 0), (0, 0), (0, 0)))
    vw = jnp.pad(v_win, ((0, 0), (NSA_WINDOW, 0), (0, 0), (0, 0)))
    tab_g = rel_tab.transpose(1, 0, 2)
    bi = jnp.arange(B)[:, None, None, None]
    gi = jnp.arange(G)[None, None, :, None]
    blk_ids = jnp.arange(n_blk)
    n_win = NSA_WINDOW + NSA_Q_BLOCK

    def block(s0):
        t = s0 + jnp.arange(NSA_Q_BLOCK)
        qb = lax.dynamic_slice_in_dim(q, s0, NSA_Q_BLOCK, axis=1)
        s_c = jnp.einsum('bqghd,bngd->bqghn', qb, k_cmp) * scale
        p_c = masked_softmax(s_c, (cmp_end[None, :] <= t[:, None])[None, :, None, None, :])
        o_c = jnp.einsum('bqghn,bngd->bqghd', p_c.astype(v_cmp.dtype), v_cmp)
        imp = jnp.pad(jnp.sum(p_c, axis=3), ((0, 0), (0, 0), (0, 0), (pad_front, r)))
        p_slc = sum(imp[..., pad_front - m - n: pad_front - m - n + r * n_blk: r]
                    for m in range(r) for n in range(c))
        blk_t = (t // SEL_BLOCK)[:, None]
        future = blk_ids[None, :] > blk_t
        forced = (blk_ids[None, :] == 0) | (blk_ids[None, :] == blk_t) | (blk_ids[None, :] == blk_t - 1)
        score = jnp.where(future[None, :, None, :], -jnp.inf,
                          jnp.where(forced[None, :, None, :], jnp.inf, p_slc))
        _, idx = lax.top_k(score, n_sel)
        ks = kb[bi, gi, idx]
        vs = vb[bi, gi, idx]
        kpos = idx[..., None] * SEL_BLOCK + jnp.arange(SEL_BLOCK)
        dist = t[None, :, None, None, None] - kpos
        bias = tab_g[gi[..., None], t5_bucket(dist)]
        s_s = jnp.einsum('bqghd,bqgnkd->bqghnk', qb, ks) * scale + jnp.moveaxis(bias, -1, 3)
        s_s = s_s.reshape(B, NSA_Q_BLOCK, G, Hg, n_sel * SEL_BLOCK)
        m_s = (dist >= 0).reshape(B, NSA_Q_BLOCK, G, 1, n_sel * SEL_BLOCK)
        p_s = masked_softmax(s_s, m_s)
        o_s = jnp.einsum('bqghm,bqgmd->bqghd', p_s.astype(vs.dtype),
                         vs.reshape(B, NSA_Q_BLOCK, G, n_sel * SEL_BLOCK, dh))
        kwb = lax.dynamic_slice_in_dim(kw, s0, n_win, axis=1)
        vwb = lax.dynamic_slice_in_dim(vw, s0, n_win, axis=1)
        kpos_w = s0 - NSA_WINDOW + jnp.arange(n_win)
        dist_w = t[:, None] - kpos_w[None, :]
        mask_w = (kpos_w[None, :] >= 0) & (dist_w >= 0) & (dist_w < NSA_WINDOW)
        bias_w = rel_tab[t5_bucket(dist_w)].transpose(0, 2, 3, 1)
        s_w = jnp.einsum('bqghd,bkgd->bqghk', qb, kwb) * scale + bias_w[None]
        p_w = masked_softmax(s_w, mask_w[None, :, None, None, :])
        o_w = jnp.einsum('bqghk,bkgd->bqghd', p_w.astype(vwb.dtype), vwb)
        return jnp.stack([o_c, o_s, o_w], axis=2)

    starts = jnp.arange(S // NSA_Q_BLOCK, dtype=jnp.int32) * NSA_Q_BLOCK
    o = lax.map(block, starts)
    o = jnp.moveaxis(o, 0, 1).reshape(B, S, 3, G, Hg, dh)
    o = jnp.sum(branch_gate[..., None] * o, axis=2)
    return o.reshape(B, S, G * Hg * dh)


def swa_attention(q, k, v, sinks, rel_tab):
    B, S, KV, Hg, dh = q.shape
    scale = HEAD_DIM ** -0.5
    n_keys = SWA_WINDOW + SWA_Q_BLOCK
    kp = jnp.pad(k, ((0, 0), (SWA_WINDOW, 0), (0, 0), (0, 0)))
    vp = jnp.pad(v, ((0, 0), (SWA_WINDOW, 0), (0, 0), (0, 0)))

    def block(s0):
        t = s0 + jnp.arange(SWA_Q_BLOCK)
        qb = lax.dynamic_slice_in_dim(q, s0, SWA_Q_BLOCK, axis=1)
        kb = lax.dynamic_slice_in_dim(kp, s0, n_keys, axis=1)
        vb = lax.dynamic_slice_in_dim(vp, s0, n_keys, axis=1)
        kpos = s0 - SWA_WINDOW + jnp.arange(n_keys)
        dist = t[:, None] - kpos[None, :]
        mask = (kpos[None, :] >= 0) & (dist >= 0) & (dist < SWA_WINDOW)
        bias = rel_tab[t5_bucket(dist)].transpose(0, 2, 3, 1)
        s = jnp.einsum('bqghd,bkgd->bqghk', qb, kb) * scale + bias[None]
        sink_col = jnp.broadcast_to(sinks[None, None, :, :, None].astype(s.dtype), s.shape[:-1] + (1,))
        s = jnp.concatenate([s, sink_col], axis=-1)
        mask = jnp.concatenate([mask[None, :, None, None, :],
                                jnp.ones((1, SWA_Q_BLOCK, 1, 1, 1), dtype=bool)], axis=-1)
        p = masked_softmax(s, mask)[..., :-1]
        return jnp.einsum('bqghk,bkgd->bqghd', p.astype(vb.dtype), vb)

    starts = jnp.arange(S // SWA_Q_BLOCK, dtype=jnp.int32) * SWA_Q_BLOCK
    o = lax.map(block, starts)
    return jnp.moveaxis(o, 0, 1).reshape(B, S, KV * Hg * dh)


def clamped_swiglu(gu):
    x_glu, x_lin = gu[..., ::2], gu[..., 1::2]
    x_glu = jnp.minimum(x_glu, SWIGLU_LIMIT)
    x_lin = jnp.clip(x_lin, -SWIGLU_LIMIT, SWIGLU_LIMIT)
    return x_glu * jax.nn.sigmoid(SWIGLU_ALPHA * x_glu) * (x_lin + 1.0)


def moe_ffn(h, w_router, b_router, w_up, b_up, w_down, b_down):
    B, S, D = h.shape
    T = B * S
    E = w_up.shape[0]
    ht = h.reshape(T, D)
    logits = (ht @ w_router + b_router).astype(jnp.float32)
    top_val, top_idx = lax.top_k(logits, TOP_K)
    gate = jax.nn.softmax(top_val, axis=-1)
    A = T * TOP_K
    e_flat = top_idx.reshape(A)
    order = jnp.argsort(e_flat)
    e_sorted = e_flat[order]
    tok_sorted = (order // TOP_K).astype(jnp.int32)
    gate_sorted = gate.reshape(A)[order].astype(h.dtype)
    counts = jnp.zeros((E,), jnp.int32).at[e_flat].add(1)
    starts = jnp.cumsum(counts) - counts
    padded = (counts + MOE_BLOCK - 1) // MOE_BLOCK * MOE_BLOCK
    pad_end = jnp.cumsum(padded)
    dest = pad_end[e_sorted] - padded[e_sorted] + jnp.arange(A, dtype=jnp.int32) - starts[e_sorted]
    n_blocks = -(-A // MOE_BLOCK) + E
    slot_tok = jnp.zeros((n_blocks * MOE_BLOCK,), jnp.int32).at[dest].set(tok_sorted)
    slot_gate = jnp.zeros((n_blocks * MOE_BLOCK,), h.dtype).at[dest].set(gate_sorted)
    block_expert = jnp.minimum(
        jnp.searchsorted(pad_end, jnp.arange(n_blocks, dtype=jnp.int32) * MOE_BLOCK, side='right'),
        E - 1).astype(jnp.int32)

    def body(y, blk):
        tok, g, e = blk
        xb = ht[tok]
        hid = clamped_swiglu(xb @ w_up[e] + b_up[e])
        out = hid @ w_down[e] + b_down[e]
        return y.at[tok].add(out * g[:, None]), None

    y, _ = lax.scan(body, jnp.zeros_like(ht),
                    (slot_tok.reshape(n_blocks, MOE_BLOCK), slot_gate.reshape(n_blocks, MOE_BLOCK), block_expert))
    return y.reshape(B, S, D)


def setup_inputs(seed: int = 0) -> dict:
    key = jax.random.key(seed)
    ks = jax.random.split(key, 24)
    f32 = jnp.float32

    def nrm(k, shape, scale):
        return jax.random.normal(k, shape, f32) * scale

    L = DEPTH
    E = N_EXPERTS
    return {
        'x': nrm(ks[0], (BATCH, SEQ, D_MODEL), 1.0),
        'c': nrm(ks[1], (BATCH, D_MODEL), 1.0),
        'w_ada': nrm(ks[2], (L, D_MODEL, 6 * D_MODEL), 0.5 * D_MODEL ** -0.5),
        'b_ada': nrm(ks[3], (L, 6 * D_MODEL), 0.02),
        'g_mix': 1.0 + nrm(ks[4], (L, D_MODEL), 0.05),
        'g_ffn': 1.0 + nrm(ks[5], (L, D_MODEL), 0.05),
        'g_final': 1.0 + nrm(ks[6], (D_MODEL,), 0.05),
        'w_in': nrm(ks[7], (L, D_MODEL, D_IN), D_MODEL ** -0.5),
        'cmp_pos': nrm(ks[8], (L, 2, CMP_LEN, HEAD_DIM), 0.1),
        'cmp_w1': nrm(ks[9], (L, 2, CMP_LEN * HEAD_DIM, CMP_HIDDEN), (CMP_LEN * HEAD_DIM) ** -0.5),
        'cmp_w2': nrm(ks[10], (L, 2, CMP_HIDDEN, HEAD_DIM), CMP_HIDDEN ** -0.5),
        'sinks': nrm(ks[11], (L, SWA_HEADS), 0.5),
        'rel_tab': nrm(ks[12], (REL_BUCKETS, N_HEADS_TOTAL), 0.2),
        'w_branch': nrm(ks[13], (L, 2, NSA_Q_W, D_MODEL), NSA_Q_W ** -0.5),
        'w_out': nrm(ks[14], (L, D_MODEL, D_MODEL), D_MODEL ** -0.5),
        'w_router': nrm(ks[15], (L, D_MODEL, E), D_MODEL ** -0.5),
        'b_router': nrm(ks[16], (L, E), 0.01),
        'w_up': nrm(ks[17], (L, E, D_MODEL, 2 * D_EXPERT), D_MODEL ** -0.5),
        'b_up': nrm(ks[18], (L, E, 2 * D_EXPERT), 0.01),
        'w_down': nrm(ks[19], (L, E, D_EXPERT, D_MODEL), D_EXPERT ** -0.5),
        'b_down': nrm(ks[20], (L, E, D_MODEL), 0.01),
    }


def reference(x, c, w_ada, b_ada, g_mix, g_ffn, g_final, w_in, cmp_pos, cmp_w1, cmp_w2, sinks, rel_tab,
              w_branch, w_out, w_router, b_router, w_up, b_up, w_down, b_down):
    B, S, D = x.shape
    G, Hg = NSA_GROUPS, NSA_HEADS // NSA_GROUPS
    KV, Hs = SWA_KV_HEADS, SWA_HEADS // SWA_KV_HEADS
    nsa_tab = rel_tab[:, :NSA_HEADS].reshape(REL_BUCKETS, G, Hg)
    swa_tab = rel_tab[:, NSA_HEADS:].reshape(REL_BUCKETS, KV, Hs)
    cond = jax.nn.silu(c)
    offs = [int(v) for v in np.cumsum(IN_SPLITS)[:-1]]
    for l in range(DEPTH):
        mod = (cond @ w_ada[l] + b_ada[l])[:, None, :]
        sh1, sc1, ga1, sh2, sc2, ga2 = jnp.split(mod, 6, axis=-1)
        h = rms_norm(x, g_mix[l]) * (1.0 + sc1) + sh1
        z = h @ w_in[l]
        q_n, kv_n, gate_n, q_s, kv_s, gate_m = jnp.split(z, offs, axis=-1)
        k_c, v_c, k_sl, v_sl, k_w, v_w = [a.reshape(B, S, G, HEAD_DIM) for a in jnp.split(kv_n, 6, axis=-1)]
        k_c = compress(k_c, cmp_pos[l, 0], cmp_w1[l, 0], cmp_w2[l, 0])
        v_c = compress(v_c, cmp_pos[l, 1], cmp_w1[l, 1], cmp_w2[l, 1])
        o_n = nsa_attention(q_n.reshape(B, S, G, Hg, HEAD_DIM), k_c, v_c, k_sl, v_sl, k_w, v_w,
                            jax.nn.sigmoid(gate_n).reshape(B, S, 3, G, Hg), nsa_tab)
        k_s, v_s = jnp.split(kv_s, 2, axis=-1)
        o_s = swa_attention(q_s.reshape(B, S, KV, Hs, HEAD_DIM), k_s.reshape(B, S, KV, HEAD_DIM),
                            v_s.reshape(B, S, KV, HEAD_DIM), sinks[l].reshape(KV, Hs), swa_tab)
        g_a, g_b = jnp.split(jax.nn.sigmoid(gate_m), 2, axis=-1)
        merged = g_a * (o_n @ w_branch[l, 0]) + g_b * (o_s @ w_branch[l, 1])
        x = x + ga1 * (merged @ w_out[l])
        h = rms_norm(x, g_ffn[l]) * (1.0 + sc2) + sh2
        x = x + ga2 * moe_ffn(h, w_router[l], b_router[l], w_up[l], b_up[l], w_down[l], b_down[l])
    return rms_norm(x, g_final)
```

```python
import functools
import math

import numpy as np
import jax
import jax.numpy as jnp
from jax import lax
from jax.experimental import pallas as pl
from jax.experimental.pallas import tpu as pltpu

D_MODEL = 1024
HEAD_DIM = 64
NSA_HEADS = 8
NSA_GROUPS = 2
HEADS_PER_GROUP = NSA_HEADS // NSA_GROUPS
CMP_LEN = 32
CMP_STRIDE = 16
CMP_HIDDEN = 256
SEL_BLOCK = 64
SEL_TOP_N = 16
NSA_WINDOW = 512
NSA_Q_BLOCK = 64
SWA_HEADS = 8
SWA_KV_HEADS = 2
SWA_WINDOW = 128
SWA_Q_BLOCK = 128
REL_BUCKETS = 32
REL_MAX_DIST = 128
N_EXPERTS = 32
TOP_K = 4
D_EXPERT = 1024
SWIGLU_ALPHA = 1.702
SWIGLU_LIMIT = 7.0
MOE_BLOCK = 256
RMS_EPS = 1e-5

LANES = 128
NEG = -1e30
FAR_TILE = 512
FAR_TILE_BLOCKS = FAR_TILE // SEL_BLOCK
SEL_NEAR = 2 * SEL_BLOCK
TAIL_KEYS = SEL_NEAR + NSA_Q_BLOCK
WIN_KEYS = NSA_WINDOW + NSA_Q_BLOCK
SWA_KEYS = SWA_WINDOW + SWA_Q_BLOCK
N_FORCED = 3
VMEM_LIMIT = 56 * 1024 * 1024

F32 = jnp.float32
BF16 = jnp.bfloat16

_OFF_QN = 0
_OFF_KVN = 512
_OFF_GN = 1280
_OFF_QS = 1304
_OFF_KVS = 1816
_OFF_GM = 2072
_D_IN = 4120
_P_QN, _P_CMP, _P_SLC, _P_WIN, _P_GN, _P_QS, _P_SWA, _P_GM, _P_END = (
    0, 512, 768, 1024, 1280, 1408, 1920, 2176, 4224)


def _sigmoid(v):
    return 1.0 / (1.0 + jnp.exp(-v))


def _dot(a, b):
    return jnp.dot(a, b, preferred_element_type=F32)


def _dot_nt(a, b):
    return lax.dot_general(a, b, (((1,), (1,)), ((), ())), preferred_element_type=F32)


def _split3(v):
    hi = v.astype(BF16)
    r1 = v - hi.astype(F32)
    mid = r1.astype(BF16)
    lo = (r1 - mid.astype(F32)).astype(BF16)
    return hi, mid, lo


def _in_perm():
    def kv_pairs(base):
        out = []
        for g in range(2):
            out += list(range(base + g * 64, base + g * 64 + 64))
            out += list(range(base + 128 + g * 64, base + 128 + g * 64 + 64))
        return out
    idx = list(range(_OFF_QN, _OFF_QN + 512))
    idx += kv_pairs(_OFF_KVN)
    idx += kv_pairs(_OFF_KVN + 256)
    idx += kv_pairs(_OFF_KVN + 512)
    idx += list(range(_OFF_GN, _OFF_GN + 24)) + [_D_IN] * (LANES - 24)
    idx += list(range(_OFF_QS, _OFF_QS + 512))
    idx += kv_pairs(_OFF_KVS)
    idx += list(range(_OFF_GM, _OFF_GM + 2048))
    assert len(idx) == _P_END
    return np.asarray(idx, np.int32)


def _ada_kernel(c_ref, w_ref, b_ref, o_ref):
    c = c_ref[...]
    cond = c * _sigmoid(c)
    o_ref[0] = _dot(cond.astype(BF16), w_ref[0].astype(BF16)) + b_ref[0]


def _ada_mod(c_pad, w_ada, b_ada):
    L, D, N = w_ada.shape
    tn = 1536
    return pl.pallas_call(
        _ada_kernel,
        grid=(L, N // tn),
        in_specs=[pl.BlockSpec((8, D), lambda l, j: (0, 0)),
                  pl.BlockSpec((1, D, tn), lambda l, j: (l, 0, j)),
                  pl.BlockSpec((1, 1, tn), lambda l, j: (l, 0, j))],
        out_specs=pl.BlockSpec((1, 8, tn), lambda l, j: (l, 0, j)),
        out_shape=jax.ShapeDtypeStruct((L, 8, N), F32),
        compiler_params=pltpu.CompilerParams(dimension_semantics=("arbitrary", "arbitrary"),
                                             vmem_limit_bytes=VMEM_LIMIT),
    )(c_pad, w_ada, b_ada.reshape(L, 1, N))


def _inproj_kernel(x_ref, mod_ref, g_ref, w_ref,
                   qn_ref, kvc_ref, kvsl_ref, kvw_ref, gn_ref, qs_ref, kvs_ref, gm_ref):
    x = x_ref[0]
    ms = jnp.mean(x * x, axis=-1, keepdims=True)
    y = x * lax.rsqrt(ms + RMS_EPS) * g_ref[...]
    h = (y * (1.0 + mod_ref[0, 1:2, :]) + mod_ref[0, 0:1, :]).astype(BF16)

    def proj(a, b):
        return _dot(h, w_ref[:, a:b])

    scale = HEAD_DIM ** -0.5
    qn_ref[0] = (proj(_P_QN, _P_CMP) * scale).astype(BF16)
    kvc_ref[0] = proj(_P_CMP, _P_SLC)
    kvsl_ref[0] = proj(_P_SLC, _P_WIN).astype(BF16)
    kvw_ref[0] = proj(_P_WIN, _P_GN).astype(BF16)
    gn_ref[0] = proj(_P_GN, _P_QS)
    qs_ref[0] = (proj(_P_QS, _P_SWA) * scale).astype(BF16)
    kvs_ref[0] = proj(_P_SWA, _P_GM).astype(BF16)
    gm_ref[0] = proj(_P_GM, _P_END)


def _inproj(x, mod, g, w):
    B, S, D = x.shape
    tm = min(512, S)
    widths = [(512, BF16), (256, F32), (256, BF16), (256, BF16), (LANES, F32), (512, BF16), (256, BF16),
              (2048, F32)]
    return pl.pallas_call(
        _inproj_kernel,
        grid=(B, S // tm),
        in_specs=[pl.BlockSpec((1, tm, D), lambda b, i: (b, i, 0)),
                  pl.BlockSpec((1, 6, D), lambda b, i: (b, 0, 0)),
                  pl.BlockSpec((1, D), lambda b, i: (0, 0)),
                  pl.BlockSpec((D, _P_END), lambda b, i: (0, 0))],
        out_specs=[pl.BlockSpec((1, tm, n), lambda b, i: (b, i, 0)) for n, _ in widths],
        out_shape=[jax.ShapeDtypeStruct((B, S, n), dt) for n, dt in widths],
        compiler_params=pltpu.CompilerParams(dimension_semantics=("arbitrary", "arbitrary"),
                                             vmem_limit_bytes=VMEM_LIMIT),
    )(x, mod, g, w)


def _gelu_tanh(v):
    return 0.5 * v * (1.0 + jnp.tanh(math.sqrt(2.0 / math.pi) * (v + 0.044715 * (v * v * v))))


def _compress_kernel(x_ref, pos_ref, w1_ref, w2_ref, o_ref):
    nch = x_ref.shape[2]
    row = lax.broadcasted_iota(jnp.int32, (nch, 1), 0)
    outs = []
    for kv in range(2):
        xc = x_ref[0, kv]
        a = _dot((xc + pos_ref[kv, 0]).astype(BF16), w1_ref[kv, 0])
        b = _dot((xc + pos_ref[kv, 1]).astype(BF16), w1_ref[kv, 1])
        b_next = pltpu.roll(b, nch - 1, axis=0)
        pre = a + jnp.where(row < nch - 1, b_next, 0.0)
        outs.append(_dot(_gelu_tanh(pre).astype(BF16), w2_ref[kv]))
    o_ref[0, 0] = jnp.concatenate(outs, axis=-1).astype(BF16)


def _compress(xc, pos, w1, w2):
    B, _, nch, cw = xc.shape
    return pl.pallas_call(
        _compress_kernel,
        grid=(B, NSA_GROUPS),
        in_specs=[pl.BlockSpec((1, 2, nch, cw), lambda b, g: (b, g, 0, 0)),
                  pl.BlockSpec((2, 2, 1, cw), lambda b, g: (0, 0, 0, 0)),
                  pl.BlockSpec((2, 2, cw, CMP_HIDDEN), lambda b, g: (0, 0, 0, 0)),
                  pl.BlockSpec((2, CMP_HIDDEN, HEAD_DIM), lambda b, g: (0, 0, 0))],
        out_specs=pl.BlockSpec((1, 1, nch, 2 * HEAD_DIM), lambda b, g: (b, g, 0, 0)),
        out_shape=jax.ShapeDtypeStruct((B, NSA_GROUPS, nch, 2 * HEAD_DIM), BF16),
        compiler_params=pltpu.CompilerParams(dimension_semantics=("arbitrary", "arbitrary"),
                                             vmem_limit_bytes=VMEM_LIMIT),
    )(xc, pos, w1, w2)


def _nsa_kernel(qn_ref, gn_ref, cmp_ref, slc_ref, win_ref, mmat_ref, btail_ref, bwin_ref, o_ref):
    qi = pl.program_id(1)
    s0 = qi * NSA_Q_BLOCK
    blk_t = qi
    ncp = cmp_ref.shape[2]
    nb = mmat_ref.shape[1]
    rows = HEADS_PER_GROUP * NSA_Q_BLOCK
    q = qn_ref[0]
    gates = _sigmoid(gn_ref[0])
    var = jnp.minimum(qi, 2)

    def stack_heads(v):
        return jnp.concatenate([v] * HEADS_PER_GROUP, axis=0)

    for g in range(NSA_GROUPS):
        qg = jnp.concatenate(
            [q[:, (g * HEADS_PER_GROUP + h) * HEAD_DIM:(g * HEADS_PER_GROUP + h + 1) * HEAD_DIM]
             for h in range(HEADS_PER_GROUP)], axis=0)
        qg2 = jnp.concatenate([qg, jnp.zeros_like(qg)], axis=1)

        kvc = cmp_ref[0, g]
        s_c = _dot_nt(qg2, kvc)
        col_c = lax.broadcasted_iota(jnp.int32, (rows, ncp), 1)
        t_c = s0 + (lax.broadcasted_iota(jnp.int32, (rows, ncp), 0) & (NSA_Q_BLOCK - 1))
        mask_c = col_c * CMP_STRIDE + (CMP_LEN - 1) <= t_c
        s_c = jnp.where(mask_c, s_c, NEG)
        m_c = jnp.max(s_c, axis=1, keepdims=True)
        e_c = jnp.where(mask_c, jnp.exp(s_c - m_c), 0.0)
        d_c = jnp.sum(e_c, axis=1, keepdims=True)
        p_c = e_c / jnp.where(d_c > 0, d_c, 1.0)
        o_c = _dot(p_c.astype(BF16), kvc)[:, HEAD_DIM:]
        imp = (p_c[0:NSA_Q_BLOCK] + p_c[NSA_Q_BLOCK:2 * NSA_Q_BLOCK]
               + p_c[2 * NSA_Q_BLOCK:3 * NSA_Q_BLOCK] + p_c[3 * NSA_Q_BLOCK:])
        mm = mmat_ref[...]
        i_hi, i_mid, i_lo = _split3(imp)
        p_slc = _dot(i_hi, mm) + _dot(i_mid, mm) + _dot(i_lo, mm)

        blk = lax.broadcasted_iota(jnp.int32, (NSA_Q_BLOCK, nb), 1)
        cand = (blk >= 1) & (blk <= blk_t - 2)
        c = jnp.where(cand, p_slc, -jnp.inf)
        sel = jnp.zeros((NSA_Q_BLOCK, nb), F32)
        for _ in range(SEL_TOP_N - N_FORCED):
            mx = jnp.max(c, axis=1, keepdims=True)
            first = jnp.min(jnp.where(c == mx, blk, nb), axis=1, keepdims=True)
            hit = blk == first
            sel = jnp.where(hit, 1.0, sel)
            c = jnp.where(hit, -jnp.inf, c)
        forced = (blk == 0) | (blk == blk_t) | (blk == blk_t - 1)
        all_causal = (blk <= blk_t) & (blk_t < SEL_TOP_N)
        sel = jnp.where(forced | all_causal, 1.0, sel)
        far_ok = (sel > 0) & (blk < blk_t - 2)
        neg_far = stack_heads(jnp.where(far_ok, 0.0, NEG).astype(BF16))
        neg_tail = stack_heads(jnp.where(sel > 0, 0.0, NEG).astype(BF16))
        q_far = jnp.concatenate([qg2, neg_far], axis=1)
        q_tail = jnp.concatenate([qg2, neg_tail], axis=1)

        n_far = (jnp.maximum(blk_t - 2, 0) + FAR_TILE_BLOCKS - 1) // FAR_TILE_BLOCKS

        def far_body(j, carry):
            m, l, acc = carry
            start = pl.multiple_of(SEL_NEAR + j * FAR_TILE, LANES)
            kv = slc_ref[0, g, pl.ds(start, FAR_TILE), :]
            s = _dot_nt(q_far, kv)
            m_new = jnp.maximum(m, jnp.max(s, axis=1, keepdims=True))
            alpha = jnp.exp(m - m_new)
            p = jnp.exp(s - m_new)
            l = alpha * l + jnp.sum(p, axis=1, keepdims=True)
            acc = alpha * acc + _dot(p.astype(BF16), kv[:, :2 * HEAD_DIM])
            return m_new, l, acc

        m, l, acc = lax.fori_loop(
            0, n_far, far_body,
            (jnp.full((rows, 1), NEG, F32), jnp.zeros((rows, 1), F32), jnp.zeros((rows, 2 * HEAD_DIM), F32)))
        kv = slc_ref[0, g, pl.ds(pl.multiple_of(s0, NSA_Q_BLOCK), TAIL_KEYS), :]
        s = _dot_nt(q_tail, kv) + btail_ref[var, g]
        m_new = jnp.maximum(m, jnp.max(s, axis=1, keepdims=True))
        alpha = jnp.exp(m - m_new)
        p = jnp.exp(s - m_new)
        l = alpha * l + jnp.sum(p, axis=1, keepdims=True)
        acc = alpha * acc + _dot(p.astype(BF16), kv[:, :2 * HEAD_DIM])
        o_s = acc[:, HEAD_DIM:] / l

        kvw = win_ref[0, g, pl.ds(pl.multiple_of(s0, NSA_Q_BLOCK), WIN_KEYS), :]
        s_w = _dot_nt(qg2, kvw) + bwin_ref[g]
        col_w = lax.broadcasted_iota(jnp.int32, (rows, WIN_KEYS), 1)
        s_w = jnp.where(col_w >= NSA_WINDOW - s0, s_w, NEG)
        m_w = jnp.max(s_w, axis=1, keepdims=True)
        p_w = jnp.exp(s_w - m_w)
        l_w = jnp.sum(p_w, axis=1, keepdims=True)
        o_w = _dot(p_w.astype(BF16), kvw)[:, HEAD_DIM:] / l_w

        for h in range(HEADS_PER_GROUP):
            head = g * HEADS_PER_GROUP + h
            r = slice(h * NSA_Q_BLOCK, (h + 1) * NSA_Q_BLOCK)
            o = (gates[:, head:head + 1] * o_c[r]
                 + gates[:, NSA_HEADS + head:NSA_HEADS + head + 1] * o_s[r]
                 + gates[:, 2 * NSA_HEADS + head:2 * NSA_HEADS + head + 1] * o_w[r])
            o_ref[0, :, head * HEAD_DIM:(head + 1) * HEAD_DIM] = o.astype(BF16)


def _nsa(qn, gn, kvcmp, slc, win, mmat, btail, bwin):
    B, S, _ = qn.shape
    G = NSA_GROUPS
    nq = S // NSA_Q_BLOCK
    qb = NSA_Q_BLOCK
    return pl.pallas_call(
        _nsa_kernel,
        grid=(B, nq),
        in_specs=[pl.BlockSpec((1, qb, 512), lambda b, i: (b, i, 0)),
                  pl.BlockSpec((1, qb, LANES), lambda b, i: (b, i, 0)),
                  pl.BlockSpec((1, G) + kvcmp.shape[2:], lambda b, i: (b, 0, 0, 0)),
                  pl.BlockSpec((1, G) + slc.shape[2:], lambda b, i: (b, 0, 0, 0)),
                  pl.BlockSpec((1, G) + win.shape[2:], lambda b, i: (b, 0, 0, 0)),
                  pl.BlockSpec(mmat.shape, lambda b, i: (0, 0)),
                  pl.BlockSpec(btail.shape, lambda b, i: (0, 0, 0, 0)),
                  pl.BlockSpec(bwin.shape, lambda b, i: (0, 0, 0))],
        out_specs=pl.BlockSpec((1, qb, 512), lambda b, i: (b, i, 0)),
        out_shape=jax.ShapeDtypeStruct((B, S, 512), BF16),
        compiler_params=pltpu.CompilerParams(dimension_semantics=("arbitrary", "arbitrary"),
                                             vmem_limit_bytes=VMEM_LIMIT),
    )(qn, gn, kvcmp, slc, win, mmat, btail, bwin)


def _swa_kernel(qs_ref, kv_ref, bias_ref, sink_ref, o_ref):
    qi = pl.program_id(1)
    s0 = qi * SWA_Q_BLOCK
    hg = SWA_HEADS // SWA_KV_HEADS
    q = qs_ref[0]
    var = jnp.minimum(qi, 1)
    for g in range(SWA_KV_HEADS):
        qg = jnp.concatenate([q[:, (g * hg + h) * HEAD_DIM:(g * hg + h + 1) * HEAD_DIM] for h in range(hg)],
                             axis=0)
        qg2 = jnp.concatenate([qg, jnp.zeros_like(qg)], axis=1)
        kv = kv_ref[0, g, pl.ds(pl.multiple_of(s0, SWA_Q_BLOCK), SWA_KEYS), :]
        s = _dot_nt(qg2, kv) + bias_ref[var, g]
        sink = sink_ref[g]
        m = jnp.maximum(jnp.max(s, axis=1, keepdims=True), sink)
        p = jnp.exp(s - m)
        l = jnp.sum(p, axis=1, keepdims=True) + jnp.exp(sink - m)
        o = _dot(p.astype(BF16), kv)[:, HEAD_DIM:] / l
        for h in range(hg):
            head = g * hg + h
            o_ref[0, :, head * HEAD_DIM:(head + 1) * HEAD_DIM] = (
                o[h * SWA_Q_BLOCK:(h + 1) * SWA_Q_BLOCK].astype(BF16))


def _swa(qs, kvp, bias, sink_rows):
    B, S, _ = qs.shape
    qb = SWA_Q_BLOCK
    return pl.pallas_call(
        _swa_kernel,
        grid=(B, S // qb),
        in_specs=[pl.BlockSpec((1, qb, 512), lambda b, i: (b, i, 0)),
                  pl.BlockSpec((1, SWA_KV_HEADS) + kvp.shape[2:], lambda b, i: (b, 0, 0, 0)),
                  pl.BlockSpec(bias.shape, lambda b, i: (0, 0, 0, 0)),
                  pl.BlockSpec(sink_rows.shape, lambda b, i: (0, 0, 0))],
        out_specs=pl.BlockSpec((1, qb, 512), lambda b, i: (b, i, 0)),
        out_shape=jax.ShapeDtypeStruct((B, S, 512), BF16),
        compiler_params=pltpu.CompilerParams(dimension_semantics=("arbitrary", "arbitrary"),
                                             vmem_limit_bytes=VMEM_LIMIT),
    )(qs, kvp, bias, sink_rows)


def _pack_bf16_pair(lo, hi):
    lo_b = pltpu.bitcast(lo.astype(BF16).astype(F32), jnp.uint32)
    hi_b = pltpu.bitcast(hi.astype(BF16).astype(F32), jnp.uint32)
    return (lo_b >> 16) | (hi_b & jnp.uint32(0xFFFF0000))


def _unpack_bf16_pair(w):
    lo = pltpu.bitcast(w << 16, F32).astype(BF16)
    hi = pltpu.bitcast(w & jnp.uint32(0xFFFF0000), F32).astype(BF16)
    return lo, hi


def _merge_kernel(on_ref, os_ref, gm_ref, x_ref, mod_ref, wb_ref, wo_ref, g_ref, wr_ref, br_ref,
                  xo_ref, hp_ref, idx_ref, gate_ref):
    d = D_MODEL
    a = _dot(on_ref[...], wb_ref[0])
    b = _dot(os_ref[...], wb_ref[1])
    gm = gm_ref[...]
    merged = _sigmoid(gm[:, :d]) * a + _sigmoid(gm[:, d:]) * b
    y = _dot(merged.astype(BF16), wo_ref[...])
    xn = x_ref[...] + mod_ref[0, 2:3, :] * y
    xo_ref[...] = xn
    ms = jnp.mean(xn * xn, axis=-1, keepdims=True)
    h2 = xn * lax.rsqrt(ms + RMS_EPS) * g_ref[...]
    h2 = h2 * (1.0 + mod_ref[0, 4:5, :]) + mod_ref[0, 3:4, :]
    hp_ref[...] = _pack_bf16_pair(h2[:, :d // 2], h2[:, d // 2:])
    h_hi, h_mid, h_lo = _split3(h2)
    w_hi, w_mid, w_lo = wr_ref[0], wr_ref[1], wr_ref[2]
    logits = (_dot(h_hi, w_hi) + (_dot(h_hi, w_mid) + _dot(h_mid, w_hi))
              + (_dot(h_hi, w_lo) + _dot(h_mid, w_mid) + _dot(h_lo, w_hi))) + br_ref[...]
    tm = logits.shape[0]
    lane = lax.broadcasted_iota(jnp.int32, (tm, LANES), 1)
    c = jnp.where(lane < N_EXPERTS, logits, -jnp.inf)
    vals, idxs = [], []
    for _ in range(TOP_K):
        mx = jnp.max(c, axis=1, keepdims=True)
        first = jnp.min(jnp.where(c == mx, lane, LANES), axis=1, keepdims=True)
        vals.append(mx)
        idxs.append(first)
        c = jnp.where(lane == first, -jnp.inf, c)
    es = [jnp.exp(v - vals[0]) for v in vals]
    den = es[0] + es[1] + es[2] + es[3]
    idx_out = jnp.zeros((tm, LANES), jnp.int32)
    gate_out = jnp.zeros((tm, LANES), F32)
    for k in range(TOP_K):
        idx_out = jnp.where(lane == k, idxs[k], idx_out)
        gate_out = jnp.where(lane == k, es[k] / den, gate_out)
    idx_ref[...] = idx_out
    gate_ref[...] = gate_out


def _merge(on, os_, gm, x, mod, wb, wo, g_ffn, wr3, br, S):
    T, D = x.shape
    tm = 256
    nb_per_seq = S // tm
    return pl.pallas_call(
        _merge_kernel,
        grid=(T // tm,),
        in_specs=[pl.BlockSpec((tm, 512), lambda i: (i, 0)),
                  pl.BlockSpec((tm, 512), lambda i: (i, 0)),
                  pl.BlockSpec((tm, 2 * D), lambda i: (i, 0)),
                  pl.BlockSpec((tm, D), lambda i: (i, 0)),
                  pl.BlockSpec((1, 6, D), lambda i: (i // nb_per_seq, 0, 0)),
                  pl.BlockSpec((2, 512, D), lambda i: (0, 0, 0)),
                  pl.BlockSpec((D, D), lambda i: (0, 0)),
                  pl.BlockSpec((1, D), lambda i: (0, 0)),
                  pl.BlockSpec((3, D, LANES), lambda i: (0, 0, 0)),
                  pl.BlockSpec((1, LANES), lambda i: (0, 0))],
        out_specs=[pl.BlockSpec((tm, D), lambda i: (i, 0)),
                   pl.BlockSpec((tm, D // 2), lambda i: (i, 0)),
                   pl.BlockSpec((tm, LANES), lambda i: (i, 0)),
                   pl.BlockSpec((tm, LANES), lambda i: (i, 0))],
        out_shape=[jax.ShapeDtypeStruct((T, D), F32),
                   jax.ShapeDtypeStruct((T, D // 2), jnp.uint32),
                   jax.ShapeDtypeStruct((T, LANES), jnp.int32),
                   jax.ShapeDtypeStruct((T, LANES), F32)],
        compiler_params=pltpu.CompilerParams(dimension_semantics=("arbitrary",),
                                             vmem_limit_bytes=VMEM_LIMIT),
    )(on, os_, gm, x, mod, wb, wo, g_ffn, wr3, br)


def _moe_kernel(bexp_ref, nval_ref,
                tok_ref, tokn_ref, dst_ref, hp_hbm, wg_ref, wl_ref, wd_ref, bg_ref, bl_ref, bd_ref,
                ys_hbm,
                xbuf, obuf, gsem, ssem):
    i = pl.program_id(0)
    nblk = pl.num_programs(0)
    slot = i % 2
    tm = MOE_BLOCK

    def issue_gather(ids_ref, sl):
        def body(r, _):
            pltpu.make_async_copy(hp_hbm.at[pl.ds(ids_ref[0, 0, r], 1), :], xbuf.at[sl, pl.ds(r, 1), :],
                                  gsem.at[sl]).start()
            return 0
        lax.fori_loop(0, tm, body, 0)

    def wait_gather(sl):
        pltpu.make_async_copy(hp_hbm.at[pl.ds(0, tm), :], xbuf.at[sl], gsem.at[sl]).wait()

    def wait_scatter(sl):
        pltpu.make_async_copy(obuf.at[sl], ys_hbm.at[pl.ds(0, tm), :], ssem.at[sl]).wait()

    @pl.when(i == 0)
    def _():
        obuf[...] = jnp.zeros_like(obuf)
        spare0 = ys_hbm.shape[0] - 2 * tm
        for sl in range(2):
            cp = pltpu.make_async_copy(obuf.at[sl], ys_hbm.at[pl.ds(spare0 + sl * tm, tm), :], ssem.at[sl])
            cp.start()
            cp.wait()

    @pl.when((i == 0) & (nval_ref[0] > 0))
    def _():
        issue_gather(tok_ref, 0)

    nxt = jnp.minimum(i + 1, nblk - 1)

    @pl.when((i + 1 < nblk) & (nval_ref[nxt] > 0))
    def _():
        issue_gather(tokn_ref, 1 - slot)

    @pl.when((i >= 2) & (nval_ref[jnp.maximum(i - 2, 0)] > 0))
    def _():
        wait_scatter(slot)

    @pl.when(nval_ref[i] > 0)
    def _():
        wait_gather(slot)
        x_lo, x_hi = _unpack_bf16_pair(xbuf[slot])
        half = D_MODEL // 2
        gu_g = _dot(x_lo, wg_ref[0, :half]) + _dot(x_hi, wg_ref[0, half:]) + bg_ref[0]
        gu_l = _dot(x_lo, wl_ref[0, :half]) + _dot(x_hi, wl_ref[0, half:]) + bl_ref[0]
        glu = jnp.minimum(gu_g, SWIGLU_LIMIT)
        lin = jnp.clip(gu_l, -SWIGLU_LIMIT, SWIGLU_LIMIT)
        hid = glu * _sigmoid(SWIGLU_ALPHA * glu) * (lin + 1.0)
        obuf[slot] = _dot(hid.astype(BF16), wd_ref[0]) + bd_ref[0]

        def body(r, _):
            pltpu.make_async_copy(obuf.at[slot, pl.ds(r, 1), :], ys_hbm.at[pl.ds(dst_ref[0, 0, r], 1), :],
                                  ssem.at[slot]).start()
            return 0
        lax.fori_loop(0, tm, body, 0)

    @pl.when(i == nblk - 1)
    def _():
        @pl.when(nval_ref[i] > 0)
        def _():
            wait_scatter(slot)

        @pl.when((nblk > 1) & (nval_ref[jnp.maximum(i - 1, 0)] > 0))
        def _():
            wait_scatter(1 - slot)


def _moe(bexp, nval, slot_tok, slot_dst, hp, wg, wl, wd, bg, bl, bd, n_rows_out):
    nblk = bexp.shape[0]
    D = D_MODEL
    tm = MOE_BLOCK
    tok3 = slot_tok.reshape(nblk, 1, tm)
    dst3 = slot_dst.reshape(nblk, 1, tm)
    smem_blk = lambda f: pl.BlockSpec((1, 1, tm), f, memory_space=pltpu.SMEM)
    grid_spec = pltpu.PrefetchScalarGridSpec(
        num_scalar_prefetch=2,
        grid=(nblk,),
        in_specs=[smem_blk(lambda i, be, nv: (i, 0, 0)),
                  smem_blk(lambda i, be, nv: (jnp.minimum(i + 1, nblk - 1), 0, 0)),
                  smem_blk(lambda i, be, nv: (i, 0, 0)),
                  pl.BlockSpec(memory_space=pl.ANY),
                  pl.BlockSpec((1, D, D_EXPERT), lambda i, be, nv: (be[i], 0, 0)),
                  pl.BlockSpec((1, D, D_EXPERT), lambda i, be, nv: (be[i], 0, 0)),
                  pl.BlockSpec((1, D_EXPERT, D), lambda i, be, nv: (be[i], 0, 0)),
                  pl.BlockSpec((1, 1, D_EXPERT), lambda i, be, nv: (be[i], 0, 0)),
                  pl.BlockSpec((1, 1, D_EXPERT), lambda i, be, nv: (be[i], 0, 0)),
                  pl.BlockSpec((1, 1, D), lambda i, be, nv: (be[i], 0, 0))],
        out_specs=pl.BlockSpec(memory_space=pl.ANY),
        scratch_shapes=[pltpu.VMEM((2, tm, D // 2), jnp.uint32),
                        pltpu.VMEM((2, tm, D), F32),
                        pltpu.SemaphoreType.DMA((2,)),
                        pltpu.SemaphoreType.DMA((2,))],
    )
    return pl.pallas_call(
        _moe_kernel,
        grid_spec=grid_spec,
        out_shape=jax.ShapeDtypeStruct((n_rows_out, D), F32),
        compiler_params=pltpu.CompilerParams(dimension_semantics=("arbitrary",),
                                             vmem_limit_bytes=VMEM_LIMIT),
    )(bexp, nval, tok3, tok3, dst3, hp, wg, wl, wd, bg, bl, bd)


def _combine_kernel(x_ref, y0_ref, y1_ref, y2_ref, y3_ref, gate_ref, mod_ref, g_ref, o_ref, *, final):
    gate = gate_ref[...]
    y = gate[:, 0:1] * y0_ref[...]
    for k, y_ref in ((1, y1_ref), (2, y2_ref), (3, y3_ref)):
        y = y + gate[:, k:k + 1] * y_ref[...]
    xn = x_ref[...] + mod_ref[0, 5:6, :] * y
    if final:
        ms = jnp.mean(xn * xn, axis=-1, keepdims=True)
        xn = xn * lax.rsqrt(ms + RMS_EPS) * g_ref[...]
    o_ref[...] = xn


def _combine(x, ys, gate, mod, g_final, S, final):
    T, D = x.shape
    tm = 256
    nb_per_seq = S // tm
    nt = T // tm
    y_specs = [pl.BlockSpec((tm, D), functools.partial(lambda i, k: (i + k * nt, 0), k=k)) for k in range(TOP_K)]
    return pl.pallas_call(
        functools.partial(_combine_kernel, final=final),
        grid=(nt,),
        in_specs=[pl.BlockSpec((tm, D), lambda i: (i, 0)),
                  *y_specs,
                  pl.BlockSpec((tm, LANES), lambda i: (i, 0)),
                  pl.BlockSpec((1, 6, D), lambda i: (i // nb_per_seq, 0, 0)),
                  pl.BlockSpec((1, D), lambda i: (0, 0))],
        out_specs=pl.BlockSpec((tm, D), lambda i: (i, 0)),
        out_shape=jax.ShapeDtypeStruct((T, D), F32),
        compiler_params=pltpu.CompilerParams(dimension_semantics=("arbitrary",),
                                             vmem_limit_bytes=VMEM_LIMIT),
    )(x, ys, ys, ys, ys, gate, mod, g_final)


def _t5_bucket(dist):
    n = jnp.maximum(dist, 0)
    max_exact = REL_BUCKETS // 2
    log_ratio = jnp.log(jnp.maximum(n, 1).astype(F32) / max_exact) / math.log(REL_MAX_DIST / max_exact)
    large = max_exact + (log_ratio * (REL_BUCKETS - max_exact)).astype(jnp.int32)
    return jnp.where(n < max_exact, n, jnp.minimum(large, REL_BUCKETS - 1))


def _toeplitz_bias(tab, q_block, n_keys, back, window, variants, shift_far):
    r = jnp.arange(q_block)[:, None]
    c = jnp.arange(n_keys)[None, :]
    dist = r + back - c
    ok = dist >= 0
    if window is not None:
        ok = ok & (dist < window)
    bias = tab[_t5_bucket(dist)]
    if shift_far:
        bias = bias - tab[REL_BUCKETS - 1][None, None]
    bias = jnp.transpose(bias, (2, 3, 0, 1))
    outs = []
    for v in range(variants):
        okv = ok if v == variants - 1 else ok & (c >= back - v * q_block)
        outs.append(jnp.where(okv[None, None], bias, NEG))
    out = jnp.stack(outs)
    V, G, Hg = out.shape[:3]
    return out.reshape(V, G, Hg * q_block, n_keys).astype(F32)


def _front_pad_groups(kv, pad):
    B, S, _ = kv.shape
    kv = kv.reshape(B, S, NSA_GROUPS, 2 * HEAD_DIM).transpose(0, 2, 1, 3)
    return jnp.pad(kv, ((0, 0), (0, 0), (pad, 0), (0, 0)))


def _routing_tables(top_idx, T):
    E = N_EXPERTS
    A = T * TOP_K
    e_flat = top_idx.reshape(A)
    order = jnp.argsort(e_flat)
    e_sorted = e_flat[order]
    tok_sorted = (order // TOP_K).astype(jnp.int32)
    k_sorted = (order % TOP_K).astype(jnp.int32)
    counts = jnp.zeros((E,), jnp.int32).at[e_flat].add(1)
    starts = jnp.cumsum(counts) - counts
    padded = (counts + MOE_BLOCK - 1) // MOE_BLOCK * MOE_BLOCK
    pad_end = jnp.cumsum(padded)
    dest = pad_end[e_sorted] - padded[e_sorted] + jnp.arange(A, dtype=jnp.int32) - starts[e_sorted]
    n_blocks = -(-A // MOE_BLOCK) + E
    slot_tok = jnp.zeros((n_blocks * MOE_BLOCK,), jnp.int32).at[dest].set(tok_sorted)
    slot_id = jnp.arange(n_blocks * MOE_BLOCK, dtype=jnp.int32)
    spare = A + slot_id % (2 * MOE_BLOCK)
    slot_dst = spare.at[dest].set(k_sorted * T + tok_sorted)
    blk_start = jnp.arange(n_blocks, dtype=jnp.int32) * MOE_BLOCK
    bexp = jnp.minimum(jnp.searchsorted(pad_end, blk_start, side='right'), E - 1).astype(jnp.int32)
    grp_start = pad_end[bexp] - padded[bexp]
    nval = jnp.clip(counts[bexp] - (blk_start - grp_start), 0, MOE_BLOCK).astype(jnp.int32)
    nval = jnp.where(blk_start < pad_end[-1], nval, 0)
    return bexp, nval, slot_tok, slot_dst


def kernel(x, c, w_ada, b_ada, g_mix, g_ffn, g_final, w_in, cmp_pos, cmp_w1, cmp_w2, sinks, rel_tab,
           w_branch, w_out, w_router, b_router, w_up, b_up, w_down, b_down):
    B, S, D = x.shape
    L = w_ada.shape[0]
    T = B * S
    G = NSA_GROUPS
    nb = S // SEL_BLOCK
    ncp = S // CMP_STRIDE

    perm = _in_perm()
    w_in_p = jnp.take(jnp.pad(w_in, ((0, 0), (0, 0), (0, 1))), perm, axis=2).astype(BF16)
    pos_r = cmp_pos.reshape(L, 2, 2, 1, CMP_STRIDE * HEAD_DIM)
    w1_r = cmp_w1.reshape(L, 2, 2, CMP_STRIDE * HEAD_DIM, CMP_HIDDEN).astype(BF16)
    w2_r = cmp_w2.astype(BF16)
    wb = w_branch.astype(BF16)
    wo = w_out.astype(BF16)
    wr_pad = jnp.pad(w_router, ((0, 0), (0, 0), (0, LANES - N_EXPERTS)))
    wr_hi = wr_pad.astype(BF16)
    wr_r1 = wr_pad - wr_hi.astype(F32)
    wr_mid = wr_r1.astype(BF16)
    wr_lo = (wr_r1 - wr_mid.astype(F32)).astype(BF16)
    wr3 = jnp.stack([wr_hi, wr_mid, wr_lo], axis=1)
    br = jnp.pad(b_router, ((0, 0), (0, LANES - N_EXPERTS))).reshape(L, 1, LANES)
    wg = w_up[..., 0::2].astype(BF16)
    wl = w_up[..., 1::2].astype(BF16)
    bg = b_up[..., 0::2].reshape(L, N_EXPERTS, 1, D_EXPERT)
    bl = b_up[..., 1::2].reshape(L, N_EXPERTS, 1, D_EXPERT)
    wd = w_down.astype(BF16)
    bd = b_down.reshape(L, N_EXPERTS, 1, D)

    nsa_tab = rel_tab[:, :NSA_HEADS].reshape(REL_BUCKETS, G, HEADS_PER_GROUP)
    swa_tab = rel_tab[:, NSA_HEADS:].reshape(REL_BUCKETS, SWA_KV_HEADS, SWA_HEADS // SWA_KV_HEADS)
    btail = _toeplitz_bias(nsa_tab, NSA_Q_BLOCK, TAIL_KEYS, SEL_NEAR, None, 3, True)
    bwin = _toeplitz_bias(nsa_tab, NSA_Q_BLOCK, WIN_KEYS, NSA_WINDOW, NSA_WINDOW, 1, False)[0]
    bswa = _toeplitz_bias(swa_tab, SWA_Q_BLOCK, SWA_KEYS, SWA_WINDOW, SWA_WINDOW, 2, False)
    ci = jnp.arange(ncp)[:, None]
    bj = jnp.arange(nb)[None, :]
    delta = (SEL_BLOCK // CMP_STRIDE) * bj - ci
    mmat = jnp.where((delta == 0) | (delta == 4), 1.0, jnp.where((delta >= 1) & (delta <= 3), 2.0, 0.0))
    mmat = jnp.where(ci < ncp - 1, mmat, 0.0).astype(BF16)
    onehot_blk = (jnp.arange(S)[:, None] // SEL_BLOCK == jnp.arange(nb)[None, :]).astype(BF16)

    c_pad = jnp.pad(c, ((0, 8 - B), (0, 0)))
    mod_all = _ada_mod(c_pad, w_ada, b_ada)[:, :B].reshape(L, B, 6, D)

    xf = x
    for l in range(L):
        mod = mod_all[l]
        qn, kvc, kvsl, kvw, gn, qs, kvs, gm = _inproj(xf, mod, g_mix[l].reshape(1, D), w_in_p[l])
        xc = kvc.reshape(B, S, 2 * G, HEAD_DIM).transpose(0, 2, 1, 3).reshape(
            B, 2 * G, ncp, CMP_STRIDE * HEAD_DIM)
        kvcmp = _compress(xc, pos_r[l], w1_r[l], w2_r[l])
        slc = kvsl.reshape(B, S, G, 2 * HEAD_DIM).transpose(0, 2, 1, 3)
        slc = jnp.concatenate([slc, jnp.broadcast_to(onehot_blk, (B, G, S, nb))], axis=-1)
        slc = jnp.pad(slc, ((0, 0), (0, 0), (SEL_NEAR, 0), (0, 0)))
        win = _front_pad_groups(kvw, NSA_WINDOW)
        o_n = _nsa(qn, gn, kvcmp, slc, win, mmat, btail, bwin)
        swa_kv = _front_pad_groups(kvs, SWA_WINDOW)
        sink_rows = jnp.repeat(sinks[l].reshape(SWA_KV_HEADS, SWA_HEADS // SWA_KV_HEADS), SWA_Q_BLOCK,
                               axis=1)[..., None]
        o_s = _swa(qs, swa_kv, bswa, sink_rows)
        x2, hp, top_idx, gate = _merge(o_n.reshape(T, 512), o_s.reshape(T, 512), gm.reshape(T, 2 * D),
                                       xf.reshape(T, D), mod, wb[l], wo[l], g_ffn[l].reshape(1, D),
                                       wr3[l], br[l], S)
        bexp, nval, slot_tok, slot_dst = _routing_tables(top_idx[:, :TOP_K], T)
        ys = _moe(bexp, nval, slot_tok, slot_dst, hp, wg[l], wl[l], wd[l], bg[l], bl[l], bd[l],
                  TOP_K * T + 2 * MOE_BLOCK)
        xf = _combine(x2, ys, gate, mod, g_final.reshape(1, D), S,
                      final=(l == L - 1)).reshape(B, S, D)
    return xf
```

```python
import functools
import math

import numpy as np
import jax
import jax.numpy as jnp
from jax import lax
from jax.experimental import pallas as pl
from jax.experimental.pallas import tpu as pltpu

D_MODEL = 1024
HEAD_DIM = 64
NSA_HEADS = 8
NSA_GROUPS = 2
HEADS_PER_GROUP = NSA_HEADS // NSA_GROUPS
CMP_LEN = 32
CMP_STRIDE = 16
CMP_HIDDEN = 256
SEL_BLOCK = 64
SEL_TOP_N = 16
NSA_WINDOW = 512
NSA_Q_BLOCK = 64
SWA_HEADS = 8
SWA_KV_HEADS = 2
SWA_WINDOW = 128
SWA_Q_BLOCK = 128
REL_BUCKETS = 32
REL_MAX_DIST = 128
N_EXPERTS = 32
TOP_K = 4
D_EXPERT = 1024
SWIGLU_ALPHA = 1.702
SWIGLU_LIMIT = 7.0
MOE_BLOCK = 256
RMS_EPS = 1e-5

LANES = 128
NEG = -1e30
FAR_TILE = 1024
FAR_TILE_BLOCKS = FAR_TILE // SEL_BLOCK
SEL_NEAR = 2 * SEL_BLOCK
TAIL_KEYS = SEL_NEAR + NSA_Q_BLOCK
WIN_KEYS = NSA_WINDOW + NSA_Q_BLOCK
SWA_KEYS = SWA_WINDOW + SWA_Q_BLOCK
N_FORCED = 3
UP_TILE = 256
VMEM_LIMIT = 56 * 1024 * 1024

F32 = jnp.float32
BF16 = jnp.bfloat16

_OFF_QN = 0
_OFF_KVN = 512
_OFF_GN = 1280
_OFF_QS = 1304
_OFF_KVS = 1816
_OFF_GM = 2072
_D_IN = 4120
_P_QN, _P_CMP, _P_SLC, _P_WIN, _P_GN, _P_QS, _P_SWA, _P_GM, _P_END = (
    0, 512, 768, 1024, 1280, 1408, 1920, 2176, 4224)


def _sigmoid(v):
    return 1.0 / (1.0 + jnp.exp(-v))


def _dot(a, b):
    return jnp.dot(a, b, preferred_element_type=F32)


def _dot_nt(a, b):
    return lax.dot_general(a, b, (((1,), (1,)), ((), ())), preferred_element_type=F32)


def _split3(v):
    hi = v.astype(BF16)
    r1 = v - hi.astype(F32)
    mid = r1.astype(BF16)
    lo = (r1 - mid.astype(F32)).astype(BF16)
    return hi, mid, lo


def _in_perm():
    def kv_pairs(base):
        out = []
        for g in range(2):
            out += list(range(base + g * 64, base + g * 64 + 64))
            out += list(range(base + 128 + g * 64, base + 128 + g * 64 + 64))
        return out
    idx = list(range(_OFF_QN, _OFF_QN + 512))
    idx += kv_pairs(_OFF_KVN)
    idx += kv_pairs(_OFF_KVN + 256)
    idx += kv_pairs(_OFF_KVN + 512)
    idx += list(range(_OFF_GN, _OFF_GN + 24)) + [_D_IN] * (LANES - 24)
    idx += list(range(_OFF_QS, _OFF_QS + 512))
    idx += kv_pairs(_OFF_KVS)
    idx += list(range(_OFF_GM, _OFF_GM + 2048))
    assert len(idx) == _P_END
    return np.asarray(idx, np.int32)


def _ada_kernel(c_ref, w_ref, b_ref, o_ref):
    c = c_ref[...]
    cond = c * _sigmoid(c)
    o_ref[0] = _dot(cond.astype(BF16), w_ref[0].astype(BF16)) + b_ref[0]


def _ada_mod(c_pad, w_ada, b_ada):
    L, D, N = w_ada.shape
    tn = 1536
    return pl.pallas_call(
        _ada_kernel,
        grid=(L, N // tn),
        in_specs=[pl.BlockSpec((8, D), lambda l, j: (0, 0)),
                  pl.BlockSpec((1, D, tn), lambda l, j: (l, 0, j)),
                  pl.BlockSpec((1, 1, tn), lambda l, j: (l, 0, j))],
        out_specs=pl.BlockSpec((1, 8, tn), lambda l, j: (l, 0, j)),
        out_shape=jax.ShapeDtypeStruct((L, 8, N), F32),
        compiler_params=pltpu.CompilerParams(dimension_semantics=("arbitrary", "arbitrary"),
                                             vmem_limit_bytes=VMEM_LIMIT),
    )(c_pad, w_ada, b_ada.reshape(L, 1, N))


def _inproj_kernel(x_ref, mod_ref, g_ref, w_ref,
                   qn_ref, kvc_ref, kvsl_ref, kvw_ref, gn_ref, qs_ref, kvs_ref, gm_ref):
    x = x_ref[0]
    ms = jnp.mean(x * x, axis=-1, keepdims=True)
    y = x * lax.rsqrt(ms + RMS_EPS) * g_ref[...]
    h = (y * (1.0 + mod_ref[0, 1:2, :]) + mod_ref[0, 0:1, :]).astype(BF16)

    def proj(a, b):
        return _dot(h, w_ref[:, a:b])

    scale = HEAD_DIM ** -0.5
    qn_ref[0] = (proj(_P_QN, _P_CMP) * scale).astype(BF16)
    kvc_ref[0] = proj(_P_CMP, _P_SLC)
    kvsl_ref[0] = proj(_P_SLC, _P_WIN).astype(BF16)
    kvw_ref[0] = proj(_P_WIN, _P_GN).astype(BF16)
    gn_ref[0] = proj(_P_GN, _P_QS)
    qs_ref[0] = (proj(_P_QS, _P_SWA) * scale).astype(BF16)
    kvs_ref[0] = proj(_P_SWA, _P_GM).astype(BF16)
    gm_ref[0] = proj(_P_GM, _P_END)


def _inproj(x, mod, g, w):
    B, S, D = x.shape
    tm = min(512, S)
    widths = [(512, BF16), (256, F32), (256, BF16), (256, BF16), (LANES, F32), (512, BF16), (256, BF16),
              (2048, F32)]
    return pl.pallas_call(
        _inproj_kernel,
        grid=(B, S // tm),
        in_specs=[pl.BlockSpec((1, tm, D), lambda b, i: (b, i, 0)),
                  pl.BlockSpec((1, 6, D), lambda b, i: (b, 0, 0)),
                  pl.BlockSpec((1, D), lambda b, i: (0, 0)),
                  pl.BlockSpec((D, _P_END), lambda b, i: (0, 0))],
        out_specs=[pl.BlockSpec((1, tm, n), lambda b, i: (b, i, 0)) for n, _ in widths],
        out_shape=[jax.ShapeDtypeStruct((B, S, n), dt) for n, dt in widths],
        compiler_params=pltpu.CompilerParams(dimension_semantics=("arbitrary", "arbitrary"),
                                             vmem_limit_bytes=VMEM_LIMIT),
    )(x, mod, g, w)


def _gelu_tanh(v):
    return 0.5 * v * (1.0 + jnp.tanh(math.sqrt(2.0 / math.pi) * (v + 0.044715 * (v * v * v))))


def _compress_kernel(x_ref, pos_ref, w1_ref, w2_ref, o_ref):
    nch = x_ref.shape[2]
    row = lax.broadcasted_iota(jnp.int32, (nch, 1), 0)
    outs = []
    for kv in range(2):
        xc = x_ref[0, kv]
        a = _dot((xc + pos_ref[kv, 0]).astype(BF16), w1_ref[kv, 0])
        b = _dot((xc + pos_ref[kv, 1]).astype(BF16), w1_ref[kv, 1])
        b_next = pltpu.roll(b, nch - 1, axis=0)
        pre = a + jnp.where(row < nch - 1, b_next, 0.0)
        outs.append(_dot(_gelu_tanh(pre).astype(BF16), w2_ref[kv]))
    o_ref[0, 0] = jnp.concatenate(outs, axis=-1).astype(BF16)


def _compress(xc, pos, w1, w2):
    B, _, nch, cw = xc.shape
    return pl.pallas_call(
        _compress_kernel,
        grid=(B, NSA_GROUPS),
        in_specs=[pl.BlockSpec((1, 2, nch, cw), lambda b, g: (b, g, 0, 0)),
                  pl.BlockSpec((2, 2, 1, cw), lambda b, g: (0, 0, 0, 0)),
                  pl.BlockSpec((2, 2, cw, CMP_HIDDEN), lambda b, g: (0, 0, 0, 0)),
                  pl.BlockSpec((2, CMP_HIDDEN, HEAD_DIM), lambda b, g: (0, 0, 0))],
        out_specs=pl.BlockSpec((1, 1, nch, 2 * HEAD_DIM), lambda b, g: (b, g, 0, 0)),
        out_shape=jax.ShapeDtypeStruct((B, NSA_GROUPS, nch, 2 * HEAD_DIM), BF16),
        compiler_params=pltpu.CompilerParams(dimension_semantics=("arbitrary", "arbitrary"),
                                             vmem_limit_bytes=VMEM_LIMIT),
    )(xc, pos, w1, w2)


def _nsa_kernel(qn_ref, gn_ref, cmp_ref, slc_ref, win_ref, mmat_ref, btail_ref, bwin_ref, o_ref):
    qi = pl.program_id(1)
    s0 = qi * NSA_Q_BLOCK
    blk_t = qi
    ncp = cmp_ref.shape[2]
    nb = mmat_ref.shape[1]
    rows = HEADS_PER_GROUP * NSA_Q_BLOCK
    q = qn_ref[0]
    gates = _sigmoid(gn_ref[0])
    var = jnp.minimum(qi, 2)

    def stack_heads(v):
        return jnp.concatenate([v] * HEADS_PER_GROUP, axis=0)

    groups = range(NSA_GROUPS)
    qg2, o_c, o_w, p_slc = [], [], [], []
    for g in groups:
        qg = jnp.concatenate(
            [q[:, (g * HEADS_PER_GROUP + h) * HEAD_DIM:(g * HEADS_PER_GROUP + h + 1) * HEAD_DIM]
             for h in range(HEADS_PER_GROUP)], axis=0)
        qg2.append(jnp.concatenate([qg, jnp.zeros_like(qg)], axis=1))

        kvc = cmp_ref[0, g]
        s_c = _dot_nt(qg2[g], kvc)
        col_c = lax.broadcasted_iota(jnp.int32, (rows, ncp), 1)
        t_c = s0 + (lax.broadcasted_iota(jnp.int32, (rows, ncp), 0) & (NSA_Q_BLOCK - 1))
        mask_c = col_c * CMP_STRIDE + (CMP_LEN - 1) <= t_c
        s_c = jnp.where(mask_c, s_c, NEG)
        m_c = jnp.max(s_c, axis=1, keepdims=True)
        e_c = jnp.where(mask_c, jnp.exp(s_c - m_c), 0.0)
        d_c = jnp.sum(e_c, axis=1, keepdims=True)
        p_c = e_c / jnp.where(d_c > 0, d_c, 1.0)
        o_c.append(_dot(p_c.astype(BF16), kvc)[:, HEAD_DIM:])
        imp = (p_c[0:NSA_Q_BLOCK] + p_c[NSA_Q_BLOCK:2 * NSA_Q_BLOCK]
               + p_c[2 * NSA_Q_BLOCK:3 * NSA_Q_BLOCK] + p_c[3 * NSA_Q_BLOCK:])
        mm = mmat_ref[...]
        i_hi, i_mid, i_lo = _split3(imp)
        p_slc.append(_dot(i_hi, mm) + _dot(i_mid, mm) + _dot(i_lo, mm))

        kvw = win_ref[0, g, pl.ds(pl.multiple_of(s0, NSA_Q_BLOCK), WIN_KEYS), :]
        s_w = _dot_nt(qg2[g], kvw) + bwin_ref[g]
        col_w = lax.broadcasted_iota(jnp.int32, (rows, WIN_KEYS), 1)
        s_w = jnp.where(col_w >= NSA_WINDOW - s0, s_w, NEG)
        m_w = jnp.max(s_w, axis=1, keepdims=True)
        p_w = jnp.exp(s_w - m_w)
        l_w = jnp.sum(p_w, axis=1, keepdims=True)
        o_w.append(_dot(p_w.astype(BF16), kvw)[:, HEAD_DIM:] / l_w)

    nq2 = NSA_GROUPS * NSA_Q_BLOCK
    p_t = jnp.concatenate(p_slc, axis=0).T
    blk = lax.broadcasted_iota(jnp.int32, (nb, nq2), 0)
    cand = (blk >= 1) & (blk <= blk_t - 2)
    c = jnp.where(cand, p_t, -jnp.inf)
    sel_t = jnp.zeros((nb, nq2), F32)
    for _ in range(SEL_TOP_N - N_FORCED):
        mx = jnp.max(c, axis=0, keepdims=True)
        first = jnp.min(jnp.where(c == mx, blk, nb), axis=0, keepdims=True)
        hit = blk == first
        sel_t = jnp.where(hit, 1.0, sel_t)
        c = jnp.where(hit, -jnp.inf, c)
    forced = (blk == 0) | (blk == blk_t) | (blk == blk_t - 1)
    all_causal = (blk <= blk_t) & (blk_t < SEL_TOP_N)
    sel_t = jnp.where(forced | all_causal, 1.0, sel_t)
    neg_tail_t = jnp.where(sel_t > 0, 0.0, NEG)
    neg_far_t = jnp.where(blk < blk_t - 2, neg_tail_t, NEG)
    neg_tail = neg_tail_t.T.astype(BF16)
    neg_far = neg_far_t.T.astype(BF16)
    q_far, q_tail = [], []
    for g in groups:
        r = slice(g * NSA_Q_BLOCK, (g + 1) * NSA_Q_BLOCK)
        q_far.append(jnp.concatenate([qg2[g], stack_heads(neg_far[r])], axis=1))
        q_tail.append(jnp.concatenate([qg2[g], stack_heads(neg_tail[r])], axis=1))

    def attend(carry, q_aug, kv, bias):
        m, l, acc = carry
        s = _dot_nt(q_aug, kv)
        if bias is not None:
            s = s + bias
        m_new = jnp.maximum(m, jnp.max(s, axis=1, keepdims=True))
        alpha = jnp.exp(m - m_new)
        p = jnp.exp(s - m_new)
        l = alpha * l + jnp.sum(p, axis=1, keepdims=True)
        acc = alpha * acc + _dot(p.astype(BF16), kv[:, :2 * HEAD_DIM])
        return m_new, l, acc

    n_far = (jnp.maximum(blk_t - 2, 0) + FAR_TILE_BLOCKS - 1) // FAR_TILE_BLOCKS

    def far_body(j, carry):
        start = pl.multiple_of(SEL_NEAR + j * FAR_TILE, LANES)
        return tuple(attend(carry[g], q_far[g], slc_ref[0, g, pl.ds(start, FAR_TILE), :], None)
                     for g in groups)

    init = (jnp.full((rows, 1), NEG, F32), jnp.zeros((rows, 1), F32), jnp.zeros((rows, 2 * HEAD_DIM), F32))
    carry = lax.fori_loop(0, n_far, far_body, (init,) * NSA_GROUPS)

    for g in groups:
        kv = slc_ref[0, g, pl.ds(pl.multiple_of(s0, NSA_Q_BLOCK), TAIL_KEYS), :]
        _, l, acc = attend(carry[g], q_tail[g], kv, btail_ref[var, g])
        o_s = acc[:, HEAD_DIM:] / l
        for h in range(HEADS_PER_GROUP):
            head = g * HEADS_PER_GROUP + h
            r = slice(h * NSA_Q_BLOCK, (h + 1) * NSA_Q_BLOCK)
            o = (gates[:, head:head + 1] * o_c[g][r]
                 + gates[:, NSA_HEADS + head:NSA_HEADS + head + 1] * o_s[r]
                 + gates[:, 2 * NSA_HEADS + head:2 * NSA_HEADS + head + 1] * o_w[g][r])
            o_ref[0, :, head * HEAD_DIM:(head + 1) * HEAD_DIM] = o.astype(BF16)


def _nsa(qn, gn, kvcmp, slc, win, mmat, btail, bwin):
    B, S, _ = qn.shape
    G = NSA_GROUPS
    nq = S // NSA_Q_BLOCK
    qb = NSA_Q_BLOCK
    return pl.pallas_call(
        _nsa_kernel,
        grid=(B, nq),
        in_specs=[pl.BlockSpec((1, qb, 512), lambda b, i: (b, i, 0)),
                  pl.BlockSpec((1, qb, LANES), lambda b, i: (b, i, 0)),
                  pl.BlockSpec((1, G) + kvcmp.shape[2:], lambda b, i: (b, 0, 0, 0)),
                  pl.BlockSpec((1, G) + slc.shape[2:], lambda b, i: (b, 0, 0, 0)),
                  pl.BlockSpec((1, G) + win.shape[2:], lambda b, i: (b, 0, 0, 0)),
                  pl.BlockSpec(mmat.shape, lambda b, i: (0, 0)),
                  pl.BlockSpec(btail.shape, lambda b, i: (0, 0, 0, 0)),
                  pl.BlockSpec(bwin.shape, lambda b, i: (0, 0, 0))],
        out_specs=pl.BlockSpec((1, qb, 512), lambda b, i: (b, i, 0)),
        out_shape=jax.ShapeDtypeStruct((B, S, 512), BF16),
        compiler_params=pltpu.CompilerParams(dimension_semantics=("arbitrary", "arbitrary"),
                                             vmem_limit_bytes=VMEM_LIMIT),
    )(qn, gn, kvcmp, slc, win, mmat, btail, bwin)


def _swa_kernel(qs_ref, kv_ref, bias_ref, sink_ref, o_ref):
    qi = pl.program_id(1)
    s0 = qi * SWA_Q_BLOCK
    hg = SWA_HEADS // SWA_KV_HEADS
    q = qs_ref[0]
    var = jnp.minimum(qi, 1)
    for g in range(SWA_KV_HEADS):
        qg = jnp.concatenate([q[:, (g * hg + h) * HEAD_DIM:(g * hg + h + 1) * HEAD_DIM] for h in range(hg)],
                             axis=0)
        qg2 = jnp.concatenate([qg, jnp.zeros_like(qg)], axis=1)
        kv = kv_ref[0, g, pl.ds(pl.multiple_of(s0, SWA_Q_BLOCK), SWA_KEYS), :]
        s = _dot_nt(qg2, kv) + bias_ref[var, g]
        sink = sink_ref[g]
        m = jnp.maximum(jnp.max(s, axis=1, keepdims=True), sink)
        p = jnp.exp(s - m)
        l = jnp.sum(p, axis=1, keepdims=True) + jnp.exp(sink - m)
        o = _dot(p.astype(BF16), kv)[:, HEAD_DIM:] / l
        for h in range(hg):
            head = g * hg + h
            o_ref[0, :, head * HEAD_DIM:(head + 1) * HEAD_DIM] = (
                o[h * SWA_Q_BLOCK:(h + 1) * SWA_Q_BLOCK].astype(BF16))


def _swa(qs, kvp, bias, sink_rows):
    B, S, _ = qs.shape
    qb = SWA_Q_BLOCK
    return pl.pallas_call(
        _swa_kernel,
        grid=(B, S // qb),
        in_specs=[pl.BlockSpec((1, qb, 512), lambda b, i: (b, i, 0)),
                  pl.BlockSpec((1, SWA_KV_HEADS) + kvp.shape[2:], lambda b, i: (b, 0, 0, 0)),
                  pl.BlockSpec(bias.shape, lambda b, i: (0, 0, 0, 0)),
                  pl.BlockSpec(sink_rows.shape, lambda b, i: (0, 0, 0))],
        out_specs=pl.BlockSpec((1, qb, 512), lambda b, i: (b, i, 0)),
        out_shape=jax.ShapeDtypeStruct((B, S, 512), BF16),
        compiler_params=pltpu.CompilerParams(dimension_semantics=("arbitrary", "arbitrary"),
                                             vmem_limit_bytes=VMEM_LIMIT),
    )(qs, kvp, bias, sink_rows)


def _pack_bf16_pair(lo, hi):
    lo_b = pltpu.bitcast(lo.astype(BF16).astype(F32), jnp.uint32)
    hi_b = pltpu.bitcast(hi.astype(BF16).astype(F32), jnp.uint32)
    return (lo_b >> 16) | (hi_b & jnp.uint32(0xFFFF0000))


def _unpack_bf16_pair(w):
    lo = pltpu.bitcast(w << 16, F32).astype(BF16)
    hi = pltpu.bitcast(w & jnp.uint32(0xFFFF0000), F32).astype(BF16)
    return lo, hi


def _merge_kernel(on_ref, os_ref, gm_ref, x_ref, mod_ref, wb_ref, wo_ref, g_ref, wr_ref, br_ref, ltri_ref,
                  xo_ref, hp_ref, idx_ref, gate_ref, cnt_ref, run_ref):
    d = D_MODEL
    a = _dot(on_ref[...], wb_ref[0])
    b = _dot(os_ref[...], wb_ref[1])
    gm = gm_ref[...]
    merged = _sigmoid(gm[:, :d]) * a + _sigmoid(gm[:, d:]) * b
    y = _dot(merged.astype(BF16), wo_ref[...])
    xn = x_ref[...] + mod_ref[0, 2:3, :] * y
    xo_ref[...] = xn
    ms = jnp.mean(xn * xn, axis=-1, keepdims=True)
    h2 = xn * lax.rsqrt(ms + RMS_EPS) * g_ref[...]
    h2 = h2 * (1.0 + mod_ref[0, 4:5, :]) + mod_ref[0, 3:4, :]
    hp_ref[...] = _pack_bf16_pair(h2[:, :d // 2], h2[:, d // 2:])
    h_hi, h_mid, h_lo = _split3(h2)
    w_hi, w_mid, w_lo = wr_ref[0], wr_ref[1], wr_ref[2]
    logits = (_dot(h_hi, w_hi) + (_dot(h_hi, w_mid) + _dot(h_mid, w_hi))
              + (_dot(h_hi, w_lo) + _dot(h_mid, w_mid) + _dot(h_lo, w_hi))) + br_ref[...]
    tm = logits.shape[0]
    lane = lax.broadcasted_iota(jnp.int32, (tm, LANES), 1)
    c = jnp.where(lane < N_EXPERTS, logits, -jnp.inf)
    vals, idxs = [], []
    for _ in range(TOP_K):
        mx = jnp.max(c, axis=1, keepdims=True)
        first = jnp.min(jnp.where(c == mx, lane, LANES), axis=1, keepdims=True)
        vals.append(mx)
        idxs.append(first)
        c = jnp.where(lane == first, -jnp.inf, c)
    es = [jnp.exp(v - vals[0]) for v in vals]
    den = es[0] + es[1] + es[2] + es[3]

    @pl.when(pl.program_id(0) == 0)
    def _():
        run_ref[...] = jnp.zeros_like(run_ref)

    onehot = jnp.zeros((tm, LANES), F32)
    for k in range(TOP_K):
        onehot = jnp.where(lane == idxs[k], 1.0, onehot)
    before = _dot(ltri_ref[...], onehot.astype(BF16)) + run_ref[...]
    run_ref[...] = run_ref[...] + jnp.sum(onehot, axis=0, keepdims=True)
    cnt_ref[...] = run_ref[...].astype(jnp.int32)

    idx_out = jnp.zeros((tm, LANES), jnp.int32)
    gate_out = jnp.zeros((tm, LANES), F32)
    for k in range(TOP_K):
        rank = jnp.sum(jnp.where(lane == idxs[k], before, 0.0), axis=1, keepdims=True).astype(jnp.int32)
        idx_out = jnp.where(lane == k, idxs[k], idx_out)
        idx_out = jnp.where(lane == TOP_K + k, rank, idx_out)
        gate_out = jnp.where(lane == k, es[k] / den, gate_out)
    idx_ref[...] = idx_out
    gate_ref[...] = gate_out


def _merge(on, os_, gm, x, mod, wb, wo, g_ffn, wr3, br, S):
    T, D = x.shape
    tm = 256
    nb_per_seq = S // tm
    ltri = jnp.asarray(np.tril(np.ones((tm, tm), np.float32), -1), BF16)
    return pl.pallas_call(
        _merge_kernel,
        grid=(T // tm,),
        in_specs=[pl.BlockSpec((tm, 512), lambda i: (i, 0)),
                  pl.BlockSpec((tm, 512), lambda i: (i, 0)),
                  pl.BlockSpec((tm, 2 * D), lambda i: (i, 0)),
                  pl.BlockSpec((tm, D), lambda i: (i, 0)),
                  pl.BlockSpec((1, 6, D), lambda i: (i // nb_per_seq, 0, 0)),
                  pl.BlockSpec((2, 512, D), lambda i: (0, 0, 0)),
                  pl.BlockSpec((D, D), lambda i: (0, 0)),
                  pl.BlockSpec((1, D), lambda i: (0, 0)),
                  pl.BlockSpec((3, D, LANES), lambda i: (0, 0, 0)),
                  pl.BlockSpec((1, LANES), lambda i: (0, 0)),
                  pl.BlockSpec((tm, tm), lambda i: (0, 0))],
        out_specs=[pl.BlockSpec((tm, D), lambda i: (i, 0)),
                   pl.BlockSpec((tm, D // 2), lambda i: (i, 0)),
                   pl.BlockSpec((tm, LANES), lambda i: (i, 0)),
                   pl.BlockSpec((tm, LANES), lambda i: (i, 0)),
                   pl.BlockSpec((1, LANES), lambda i: (0, 0))],
        out_shape=[jax.ShapeDtypeStruct((T, D), F32),
                   jax.ShapeDtypeStruct((T, D // 2), jnp.uint32),
                   jax.ShapeDtypeStruct((T, LANES), jnp.int32),
                   jax.ShapeDtypeStruct((T, LANES), F32),
                   jax.ShapeDtypeStruct((1, LANES), jnp.int32)],
        scratch_shapes=[pltpu.VMEM((1, LANES), F32)],
        compiler_params=pltpu.CompilerParams(dimension_semantics=("arbitrary",),
                                             vmem_limit_bytes=VMEM_LIMIT),
    )(on, os_, gm, x, mod, wb, wo, g_ffn, wr3, br, ltri)


def _dispatch_kernel(dest_ref, hp_ref, xs_in_hbm, xs_hbm, sem):
    del xs_in_hbm
    tm = hp_ref.shape[0]

    def body(r, _):
        for k in range(TOP_K):
            pltpu.make_async_copy(hp_ref.at[pl.ds(r, 1), :],
                                  xs_hbm.at[pl.ds(dest_ref[0, 0, r * TOP_K + k], 1), :], sem).start()
        return 0
    lax.fori_loop(0, tm, body, 0)
    for _ in range(TOP_K):
        pltpu.make_async_copy(hp_ref, xs_hbm.at[pl.ds(0, tm), :], sem).wait()


def _dispatch(dest, hp, n_slots):
    T, W = hp.shape
    tm = 256
    xs0 = jnp.zeros((n_slots, W), hp.dtype)
    return pl.pallas_call(
        _dispatch_kernel,
        grid=(T // tm,),
        in_specs=[pl.BlockSpec((1, 1, tm * TOP_K), lambda i: (i, 0, 0), memory_space=pltpu.SMEM),
                  pl.BlockSpec((tm, W), lambda i: (i, 0)),
                  pl.BlockSpec(memory_space=pl.ANY)],
        out_specs=pl.BlockSpec(memory_space=pl.ANY),
        out_shape=jax.ShapeDtypeStruct((n_slots, W), hp.dtype),
        scratch_shapes=[pltpu.SemaphoreType.DMA(())],
        input_output_aliases={2: 0},
        compiler_params=pltpu.CompilerParams(dimension_semantics=("arbitrary",),
                                             vmem_limit_bytes=VMEM_LIMIT),
    )(dest.reshape(T // tm, 1, tm * TOP_K), hp, xs0)


def _moe_kernel(bexp_ref, used_ref, xs_ref, wu_ref, wd_ref, bu_ref, bd_ref, ys_ref):
    i = pl.program_id(0)

    @pl.when(used_ref[i] > 0)
    def _():
        x_lo, x_hi = _unpack_bf16_pair(xs_ref[...])
        half = D_MODEL // 2
        hid = []
        for t in range(2 * D_EXPERT // UP_TILE):
            cols = slice(t * UP_TILE, (t + 1) * UP_TILE)
            gu = _dot(x_lo, wu_ref[0, :half, cols]) + _dot(x_hi, wu_ref[0, half:, cols]) + bu_ref[0, :, cols]
            glu = jnp.minimum(gu[:, :UP_TILE // 2], SWIGLU_LIMIT)
            lin = jnp.clip(gu[:, UP_TILE // 2:], -SWIGLU_LIMIT, SWIGLU_LIMIT)
            hid.append((glu * _sigmoid(SWIGLU_ALPHA * glu) * (lin + 1.0)).astype(BF16))
        ys_ref[...] = _dot(jnp.concatenate(hid, axis=1), wd_ref[0]) + bd_ref[0]

    @pl.when(used_ref[i] == 0)
    def _():
        ys_ref[...] = jnp.zeros_like(ys_ref)


def _wup_prep_kernel(w_ref, p_ref, o_ref):
    for t in range(w_ref.shape[2] // UP_TILE):
        cols = slice(t * UP_TILE, (t + 1) * UP_TILE)
        o_ref[0, :, cols] = _dot(w_ref[0, :, cols].astype(BF16), p_ref[...]).astype(BF16)


def _wup_prep(w_up):
    N, D, W = w_up.shape
    j = np.arange(UP_TILE)
    perm = np.zeros((UP_TILE, UP_TILE), np.float32)
    perm[j, (j % 2) * (UP_TILE // 2) + j // 2] = 1.0
    return pl.pallas_call(
        _wup_prep_kernel,
        grid=(N,),
        in_specs=[pl.BlockSpec((1, D, W), lambda n: (n, 0, 0)),
                  pl.BlockSpec((UP_TILE, UP_TILE), lambda n: (0, 0))],
        out_specs=pl.BlockSpec((1, D, W), lambda n: (n, 0, 0)),
        out_shape=jax.ShapeDtypeStruct((N, D, W), BF16),
        compiler_params=pltpu.CompilerParams(dimension_semantics=("arbitrary",),
                                             vmem_limit_bytes=VMEM_LIMIT),
    )(w_up, jnp.asarray(perm, BF16))


def _moe(bexp, used, xs, wu, wd, bu, bd):
    nblk = bexp.shape[0]
    D = D_MODEL
    tm = MOE_BLOCK
    grid_spec = pltpu.PrefetchScalarGridSpec(
        num_scalar_prefetch=2,
        grid=(nblk,),
        in_specs=[pl.BlockSpec((tm, D // 2), lambda i, be, us: (i, 0)),
                  pl.BlockSpec((1, D, 2 * D_EXPERT), lambda i, be, us: (be[i], 0, 0)),
                  pl.BlockSpec((1, D_EXPERT, D), lambda i, be, us: (be[i], 0, 0)),
                  pl.BlockSpec((1, 1, 2 * D_EXPERT), lambda i, be, us: (be[i], 0, 0)),
                  pl.BlockSpec((1, 1, D), lambda i, be, us: (be[i], 0, 0))],
        out_specs=pl.BlockSpec((tm, D), lambda i, be, us: (i, 0)),
    )
    return pl.pallas_call(
        _moe_kernel,
        grid_spec=grid_spec,
        out_shape=jax.ShapeDtypeStruct((nblk * tm, D), F32),
        compiler_params=pltpu.CompilerParams(dimension_semantics=("arbitrary",),
                                             vmem_limit_bytes=VMEM_LIMIT),
    )(bexp, used, xs, wu, wd, bu, bd)


def _combine_kernel(dest_ref, destn_ref, x_ref, gate_ref, mod_ref, g_ref, ys_hbm, o_ref, ybuf, sem, *, final):
    i = pl.program_id(0)
    nt = pl.num_programs(0)
    slot = i % 2
    tm = x_ref.shape[0]

    def issue(ids_ref, sl):
        def body(r, _):
            for k in range(TOP_K):
                pltpu.make_async_copy(ys_hbm.at[pl.ds(ids_ref[0, 0, r * TOP_K + k], 1), :],
                                      ybuf.at[sl, k, pl.ds(r, 1), :], sem.at[sl]).start()
            return 0
        lax.fori_loop(0, tm, body, 0)

    @pl.when(i == 0)
    def _():
        issue(dest_ref, 0)

    @pl.when(i + 1 < nt)
    def _():
        issue(destn_ref, 1 - slot)

    for k in range(TOP_K):
        pltpu.make_async_copy(ys_hbm.at[pl.ds(0, tm), :], ybuf.at[slot, k], sem.at[slot]).wait()

    gate = gate_ref[...]
    y = gate[:, 0:1] * ybuf[slot, 0]
    for k in range(1, TOP_K):
        y = y + gate[:, k:k + 1] * ybuf[slot, k]
    xn = x_ref[...] + mod_ref[0, 5:6, :] * y
    if final:
        ms = jnp.mean(xn * xn, axis=-1, keepdims=True)
        xn = xn * lax.rsqrt(ms + RMS_EPS) * g_ref[...]
    o_ref[...] = xn


def _combine(dest, x, ys, gate, mod, g_final, S, final):
    T, D = x.shape
    tm = 256
    nb_per_seq = S // tm
    nt = T // tm
    dest3 = dest.reshape(nt, 1, tm * TOP_K)
    smem_blk = lambda f: pl.BlockSpec((1, 1, tm * TOP_K), f, memory_space=pltpu.SMEM)
    return pl.pallas_call(
        functools.partial(_combine_kernel, final=final),
        grid=(nt,),
        in_specs=[smem_blk(lambda i: (i, 0, 0)),
                  smem_blk(lambda i: (jnp.minimum(i + 1, nt - 1), 0, 0)),
                  pl.BlockSpec((tm, D), lambda i: (i, 0)),
                  pl.BlockSpec((tm, LANES), lambda i: (i, 0)),
                  pl.BlockSpec((1, 6, D), lambda i: (i // nb_per_seq, 0, 0)),
                  pl.BlockSpec((1, D), lambda i: (0, 0)),
                  pl.BlockSpec(memory_space=pl.ANY)],
        out_specs=pl.BlockSpec((tm, D), lambda i: (i, 0)),
        out_shape=jax.ShapeDtypeStruct((T, D), F32),
        scratch_shapes=[pltpu.VMEM((2, TOP_K, tm, D), F32),
                        pltpu.SemaphoreType.DMA((2,))],
        compiler_params=pltpu.CompilerParams(dimension_semantics=("arbitrary",),
                                             vmem_limit_bytes=VMEM_LIMIT),
    )(dest3, dest3, x, gate, mod, g_final, ys)


def _t5_bucket(dist):
    n = jnp.maximum(dist, 0)
    max_exact = REL_BUCKETS // 2
    log_ratio = jnp.log(jnp.maximum(n, 1).astype(F32) / max_exact) / math.log(REL_MAX_DIST / max_exact)
    large = max_exact + (log_ratio * (REL_BUCKETS - max_exact)).astype(jnp.int32)
    return jnp.where(n < max_exact, n, jnp.minimum(large, REL_BUCKETS - 1))


def _toeplitz_bias(tab, q_block, n_keys, back, window, variants, shift_far):
    r = jnp.arange(q_block)[:, None]
    c = jnp.arange(n_keys)[None, :]
    dist = r + back - c
    ok = dist >= 0
    if window is not None:
        ok = ok & (dist < window)
    bias = tab[_t5_bucket(dist)]
    if shift_far:
        bias = bias - tab[REL_BUCKETS - 1][None, None]
    bias = jnp.transpose(bias, (2, 3, 0, 1))
    outs = []
    for v in range(variants):
        okv = ok if v == variants - 1 else ok & (c >= back - v * q_block)
        outs.append(jnp.where(okv[None, None], bias, NEG))
    out = jnp.stack(outs)
    V, G, Hg = out.shape[:3]
    return out.reshape(V, G, Hg * q_block, n_keys).astype(F32)


def _front_pad_groups(kv, pad):
    B, S, _ = kv.shape
    kv = kv.reshape(B, S, NSA_GROUPS, 2 * HEAD_DIM).transpose(0, 2, 1, 3)
    return jnp.pad(kv, ((0, 0), (0, 0), (pad, 0), (0, 0)))


def _routing_tables(route, counts, T):
    E = N_EXPERTS
    n_blocks = -(-(T * TOP_K) // MOE_BLOCK) + E
    padded = (counts + MOE_BLOCK - 1) // MOE_BLOCK * MOE_BLOCK
    pad_end = jnp.cumsum(padded)
    grp_start = pad_end - padded
    e_idx = route[:, :TOP_K]
    onehot = e_idx[..., None] == jnp.arange(E, dtype=jnp.int32)
    dest = jnp.sum(jnp.where(onehot, grp_start, 0), axis=-1) + route[:, TOP_K:2 * TOP_K]
    blk_start = jnp.arange(n_blocks, dtype=jnp.int32) * MOE_BLOCK
    bexp = jnp.minimum(jnp.sum(pad_end[None, :] <= blk_start[:, None], axis=1), E - 1).astype(jnp.int32)
    used = (blk_start < pad_end[-1]).astype(jnp.int32)
    return dest.astype(jnp.int32), bexp, used


def kernel(x, c, w_ada, b_ada, g_mix, g_ffn, g_final, w_in, cmp_pos, cmp_w1, cmp_w2, sinks, rel_tab,
           w_branch, w_out, w_router, b_router, w_up, b_up, w_down, b_down):
    B, S, D = x.shape
    L = w_ada.shape[0]
    T = B * S
    G = NSA_GROUPS
    nb = S // SEL_BLOCK
    ncp = S // CMP_STRIDE

    perm = _in_perm()
    w_in_p = jnp.take(jnp.pad(w_in, ((0, 0), (0, 0), (0, 1))), perm, axis=2).astype(BF16)
    pos_r = cmp_pos.reshape(L, 2, 2, 1, CMP_STRIDE * HEAD_DIM)
    w1_r = cmp_w1.reshape(L, 2, 2, CMP_STRIDE * HEAD_DIM, CMP_HIDDEN).astype(BF16)
    w2_r = cmp_w2.astype(BF16)
    wb = w_branch.astype(BF16)
    wo = w_out.astype(BF16)
    wr_pad = jnp.pad(w_router, ((0, 0), (0, 0), (0, LANES - N_EXPERTS)))
    wr_hi = wr_pad.astype(BF16)
    wr_r1 = wr_pad - wr_hi.astype(F32)
    wr_mid = wr_r1.astype(BF16)
    wr_lo = (wr_r1 - wr_mid.astype(F32)).astype(BF16)
    wr3 = jnp.stack([wr_hi, wr_mid, wr_lo], axis=1)
    br = jnp.pad(b_router, ((0, 0), (0, LANES - N_EXPERTS))).reshape(L, 1, LANES)
    wu = _wup_prep(w_up.reshape(L * N_EXPERTS, D, 2 * D_EXPERT)).reshape(L, N_EXPERTS, D, 2 * D_EXPERT)
    n_up_tiles = 2 * D_EXPERT // UP_TILE
    bu = b_up.reshape(L, N_EXPERTS, n_up_tiles, UP_TILE // 2, 2).transpose(0, 1, 2, 4, 3).reshape(
        L, N_EXPERTS, 1, 2 * D_EXPERT)
    wd = w_down.astype(BF16)
    bd = b_down.reshape(L, N_EXPERTS, 1, D)

    nsa_tab = rel_tab[:, :NSA_HEADS].reshape(REL_BUCKETS, G, HEADS_PER_GROUP)
    swa_tab = rel_tab[:, NSA_HEADS:].reshape(REL_BUCKETS, SWA_KV_HEADS, SWA_HEADS // SWA_KV_HEADS)
    btail = _toeplitz_bias(nsa_tab, NSA_Q_BLOCK, TAIL_KEYS, SEL_NEAR, None, 3, True)
    bwin = _toeplitz_bias(nsa_tab, NSA_Q_BLOCK, WIN_KEYS, NSA_WINDOW, NSA_WINDOW, 1, False)[0]
    bswa = _toeplitz_bias(swa_tab, SWA_Q_BLOCK, SWA_KEYS, SWA_WINDOW, SWA_WINDOW, 2, False)
    ci = jnp.arange(ncp)[:, None]
    bj = jnp.arange(nb)[None, :]
    delta = (SEL_BLOCK // CMP_STRIDE) * bj - ci
    mmat = jnp.where((delta == 0) | (delta == 4), 1.0, jnp.where((delta >= 1) & (delta <= 3), 2.0, 0.0))
    mmat = jnp.where(ci < ncp - 1, mmat, 0.0).astype(BF16)
    onehot_blk = (jnp.arange(S)[:, None] // SEL_BLOCK == jnp.arange(nb)[None, :]).astype(BF16)

    c_pad = jnp.pad(c, ((0, 8 - B), (0, 0)))
    mod_all = _ada_mod(c_pad, w_ada, b_ada)[:, :B].reshape(L, B, 6, D)

    xf = x
    for l in range(L):
        mod = mod_all[l]
        qn, kvc, kvsl, kvw, gn, qs, kvs, gm = _inproj(xf, mod, g_mix[l].reshape(1, D), w_in_p[l])
        xc = kvc.reshape(B, S, 2 * G, HEAD_DIM).transpose(0, 2, 1, 3).reshape(
            B, 2 * G, ncp, CMP_STRIDE * HEAD_DIM)
        kvcmp = _compress(xc, pos_r[l], w1_r[l], w2_r[l])
        slc = kvsl.reshape(B, S, G, 2 * HEAD_DIM).transpose(0, 2, 1, 3)
        slc = jnp.concatenate([slc, jnp.broadcast_to(onehot_blk, (B, G, S, nb))], axis=-1)
        slc = jnp.pad(slc, ((0, 0), (0, 0), (SEL_NEAR, 0), (0, 0)))
        win = _front_pad_groups(kvw, NSA_WINDOW)
        o_n = _nsa(qn, gn, kvcmp, slc, win, mmat, btail, bwin)
        swa_kv = _front_pad_groups(kvs, SWA_WINDOW)
        sink_rows = jnp.repeat(sinks[l].reshape(SWA_KV_HEADS, SWA_HEADS // SWA_KV_HEADS), SWA_Q_BLOCK,
                               axis=1)[..., None]
        o_s = _swa(qs, swa_kv, bswa, sink_rows)
        x2, hp, route, gate, counts = _merge(o_n.reshape(T, 512), o_s.reshape(T, 512), gm.reshape(T, 2 * D),
                                             xf.reshape(T, D), mod, wb[l], wo[l], g_ffn[l].reshape(1, D),
                                             wr3[l], br[l], S)
        dest, bexp, used = _routing_tables(route, counts[0, :N_EXPERTS], T)
        xs = _dispatch(dest, hp, bexp.shape[0] * MOE_BLOCK)
        ys = _moe(bexp, used, xs, wu[l], wd[l], bu[l], bd[l])
        xf = _combine(dest, x2, ys, gate, mod, g_final.reshape(1, D), S,
                      final=(l == L - 1)).reshape(B, S, D)
    return xf
```

```python
import functools
import math

import numpy as np
import jax
import jax.numpy as jnp
from jax import lax
from jax.experimental import pallas as pl
from jax.experimental.pallas import tpu as pltpu

D_MODEL = 1024
HEAD_DIM = 64
NSA_HEADS = 8
NSA_GROUPS = 2
HEADS_PER_GROUP = NSA_HEADS // NSA_GROUPS
CMP_LEN = 32
CMP_STRIDE = 16
CMP_HIDDEN = 256
SEL_BLOCK = 64
SEL_TOP_N = 16
NSA_WINDOW = 512
NSA_Q_BLOCK = 64
SWA_HEADS = 8
SWA_KV_HEADS = 2
SWA_WINDOW = 128
SWA_Q_BLOCK = 128
REL_BUCKETS = 32
REL_MAX_DIST = 128
N_EXPERTS = 32
TOP_K = 4
D_EXPERT = 1024
SWIGLU_ALPHA = 1.702
SWIGLU_LIMIT = 7.0
MOE_BLOCK = 256
RMS_EPS = 1e-5

LANES = 128
NEG = -1e30
FAR_TILE = 1024
FAR_TILE_BLOCKS = FAR_TILE // SEL_BLOCK
SEL_NEAR = 2 * SEL_BLOCK
TAIL_KEYS = SEL_NEAR + NSA_Q_BLOCK
WIN_KEYS = NSA_WINDOW + NSA_Q_BLOCK
SWA_KEYS = SWA_WINDOW + SWA_Q_BLOCK
N_FORCED = 3
UP_TILE = 256
DMA_LOOP_UNROLL = 8
VMEM_LIMIT = 56 * 1024 * 1024

F32 = jnp.float32
BF16 = jnp.bfloat16

_OFF_QN = 0
_OFF_KVN = 512
_OFF_GN = 1280
_OFF_QS = 1304
_OFF_KVS = 1816
_OFF_GM = 2072
_D_IN = 4120
_P_QN, _P_CMP, _P_SLC, _P_WIN, _P_GN, _P_QS, _P_SWA, _P_GM, _P_END = (
    0, 512, 768, 1024, 1280, 1408, 1920, 2176, 4224)


def _sigmoid(v):
    return 1.0 / (1.0 + jnp.exp(-v))


def _dot(a, b):
    return jnp.dot(a, b, preferred_element_type=F32)


def _dot_nt(a, b):
    return lax.dot_general(a, b, (((1,), (1,)), ((), ())), preferred_element_type=F32)


def _split3(v):
    hi = v.astype(BF16)
    r1 = v - hi.astype(F32)
    mid = r1.astype(BF16)
    lo = (r1 - mid.astype(F32)).astype(BF16)
    return hi, mid, lo


def _in_perm():
    def kv_pairs(base):
        out = []
        for g in range(2):
            out += list(range(base + g * 64, base + g * 64 + 64))
            out += list(range(base + 128 + g * 64, base + 128 + g * 64 + 64))
        return out
    idx = list(range(_OFF_QN, _OFF_QN + 512))
    idx += kv_pairs(_OFF_KVN)
    idx += kv_pairs(_OFF_KVN + 256)
    idx += kv_pairs(_OFF_KVN + 512)
    idx += list(range(_OFF_GN, _OFF_GN + 24)) + [_D_IN] * (LANES - 24)
    idx += list(range(_OFF_QS, _OFF_QS + 512))
    idx += kv_pairs(_OFF_KVS)
    idx += list(range(_OFF_GM, _OFF_GM + 2048))
    assert len(idx) == _P_END
    return np.asarray(idx, np.int32)


def _ada_kernel(c_ref, w_ref, b_ref, o_ref):
    c = c_ref[...]
    cond = c * _sigmoid(c)
    o_ref[0] = _dot(cond.astype(BF16), w_ref[0].astype(BF16)) + b_ref[0]


def _ada_mod(c_pad, w_ada, b_ada):
    L, D, N = w_ada.shape
    tn = 1536
    return pl.pallas_call(
        _ada_kernel,
        grid=(L, N // tn),
        in_specs=[pl.BlockSpec((8, D), lambda l, j: (0, 0)),
                  pl.BlockSpec((1, D, tn), lambda l, j: (l, 0, j)),
                  pl.BlockSpec((1, 1, tn), lambda l, j: (l, 0, j))],
        out_specs=pl.BlockSpec((1, 8, tn), lambda l, j: (l, 0, j)),
        out_shape=jax.ShapeDtypeStruct((L, 8, N), F32),
        compiler_params=pltpu.CompilerParams(dimension_semantics=("arbitrary", "arbitrary"),
                                             vmem_limit_bytes=VMEM_LIMIT),
    )(c_pad, w_ada, b_ada.reshape(L, 1, N))


def _inproj_kernel(x_ref, mod_ref, g_ref, w_ref,
                   qn_ref, kvc_ref, kvsl_ref, kvw_ref, gn_ref, qs_ref, kvs_ref, gm_ref):
    x = x_ref[0]
    ms = jnp.mean(x * x, axis=-1, keepdims=True)
    y = x * lax.rsqrt(ms + RMS_EPS) * g_ref[...]
    h = (y * (1.0 + mod_ref[0, 1:2, :]) + mod_ref[0, 0:1, :]).astype(BF16)

    def proj(a, b):
        return _dot(h, w_ref[:, a:b])

    scale = HEAD_DIM ** -0.5
    qn_ref[0] = (proj(_P_QN, _P_CMP) * scale).astype(BF16)
    kvc_ref[0] = proj(_P_CMP, _P_SLC)
    kvsl_ref[0] = proj(_P_SLC, _P_WIN).astype(BF16)
    kvw_ref[0] = proj(_P_WIN, _P_GN).astype(BF16)
    gn_ref[0] = proj(_P_GN, _P_QS)
    qs_ref[0] = (proj(_P_QS, _P_SWA) * scale).astype(BF16)
    kvs_ref[0] = proj(_P_SWA, _P_GM).astype(BF16)
    gm_ref[0] = proj(_P_GM, _P_END)


def _inproj(x, mod, g, w):
    B, S, D = x.shape
    tm = min(512, S)
    widths = [(512, BF16), (256, F32), (256, BF16), (256, BF16), (LANES, F32), (512, BF16), (256, BF16),
              (2048, F32)]
    return pl.pallas_call(
        _inproj_kernel,
        grid=(B, S // tm),
        in_specs=[pl.BlockSpec((1, tm, D), lambda b, i: (b, i, 0)),
                  pl.BlockSpec((1, 6, D), lambda b, i: (b, 0, 0)),
                  pl.BlockSpec((1, D), lambda b, i: (0, 0)),
                  pl.BlockSpec((D, _P_END), lambda b, i: (0, 0))],
        out_specs=[pl.BlockSpec((1, tm, n), lambda b, i: (b, i, 0)) for n, _ in widths],
        out_shape=[jax.ShapeDtypeStruct((B, S, n), dt) for n, dt in widths],
        compiler_params=pltpu.CompilerParams(dimension_semantics=("arbitrary", "arbitrary"),
                                             vmem_limit_bytes=VMEM_LIMIT),
    )(x, mod, g, w)


def _gelu_tanh(v):
    return 0.5 * v * (1.0 + jnp.tanh(math.sqrt(2.0 / math.pi) * (v + 0.044715 * (v * v * v))))


def _compress_kernel(x_ref, pos_ref, w1_ref, w2_ref, o_ref):
    nch = x_ref.shape[2]
    row = lax.broadcasted_iota(jnp.int32, (nch, 1), 0)
    outs = []
    for kv in range(2):
        xc = x_ref[0, kv]
        a = _dot((xc + pos_ref[kv, 0]).astype(BF16), w1_ref[kv, 0])
        b = _dot((xc + pos_ref[kv, 1]).astype(BF16), w1_ref[kv, 1])
        b_next = pltpu.roll(b, nch - 1, axis=0)
        pre = a + jnp.where(row < nch - 1, b_next, 0.0)
        outs.append(_dot(_gelu_tanh(pre).astype(BF16), w2_ref[kv]))
    o_ref[0, 0] = jnp.concatenate(outs, axis=-1).astype(BF16)


def _compress(xc, pos, w1, w2):
    B, _, nch, cw = xc.shape
    return pl.pallas_call(
        _compress_kernel,
        grid=(B, NSA_GROUPS),
        in_specs=[pl.BlockSpec((1, 2, nch, cw), lambda b, g: (b, g, 0, 0)),
                  pl.BlockSpec((2, 2, 1, cw), lambda b, g: (0, 0, 0, 0)),
                  pl.BlockSpec((2, 2, cw, CMP_HIDDEN), lambda b, g: (0, 0, 0, 0)),
                  pl.BlockSpec((2, CMP_HIDDEN, HEAD_DIM), lambda b, g: (0, 0, 0))],
        out_specs=pl.BlockSpec((1, 1, nch, 2 * HEAD_DIM), lambda b, g: (b, g, 0, 0)),
        out_shape=jax.ShapeDtypeStruct((B, NSA_GROUPS, nch, 2 * HEAD_DIM), BF16),
        compiler_params=pltpu.CompilerParams(dimension_semantics=("arbitrary", "arbitrary"),
                                             vmem_limit_bytes=VMEM_LIMIT),
    )(xc, pos, w1, w2)


def _nsa_kernel(qn_ref, gn_ref, cmp_ref, slc_ref, win_ref, mmat_ref, btail_ref, bwin_ref, o_ref):
    qi = pl.program_id(1)
    s0 = qi * NSA_Q_BLOCK
    blk_t = qi
    ncp = cmp_ref.shape[2]
    nb = mmat_ref.shape[1]
    rows = HEADS_PER_GROUP * NSA_Q_BLOCK
    q = qn_ref[0]
    gates = _sigmoid(gn_ref[0])
    var = jnp.minimum(qi, 2)

    def stack_heads(v):
        return jnp.concatenate([v] * HEADS_PER_GROUP, axis=0)

    groups = range(NSA_GROUPS)
    col_c = lax.broadcasted_iota(jnp.int32, (NSA_Q_BLOCK, ncp), 1)
    t_c = s0 + lax.broadcasted_iota(jnp.int32, (NSA_Q_BLOCK, ncp), 0)
    bias_c = stack_heads(jnp.where(col_c * CMP_STRIDE + (CMP_LEN - 1) <= t_c, 0.0, NEG))
    col_w = lax.broadcasted_iota(jnp.int32, (1, WIN_KEYS), 1)
    front_w = jnp.where(col_w >= NSA_WINDOW - s0, 0.0, NEG)
    def group_rows(g):
        qg = jnp.concatenate(
            [q[:, (g * HEADS_PER_GROUP + h) * HEAD_DIM:(g * HEADS_PER_GROUP + h + 1) * HEAD_DIM]
             for h in range(HEADS_PER_GROUP)], axis=0)
        return jnp.concatenate([qg, jnp.zeros_like(qg)], axis=1)

    qg2 = [group_rows(g) for g in groups]

    kvc = [cmp_ref[0, g] for g in groups]
    s_c = [_dot_nt(qg2[g], kvc[g]) + bias_c for g in groups]
    m_c = [jnp.max(s_c[g], axis=1, keepdims=True) for g in groups]
    e_c = [jnp.exp(s_c[g] - m_c[g]) for g in groups]
    d_c = [jnp.sum(e_c[g], axis=1, keepdims=True) for g in groups]
    p_c = [e_c[g] * jnp.where(m_c[g] > 0.5 * NEG, 1.0 / d_c[g], 0.0) for g in groups]
    o_c = [_dot(p_c[g].astype(BF16), kvc[g])[:, HEAD_DIM:] for g in groups]
    imp = [(p_c[g][0:NSA_Q_BLOCK] + p_c[g][NSA_Q_BLOCK:2 * NSA_Q_BLOCK]
            + p_c[g][2 * NSA_Q_BLOCK:3 * NSA_Q_BLOCK] + p_c[g][3 * NSA_Q_BLOCK:]) for g in groups]
    i_hi, i_mid, i_lo = _split3(jnp.concatenate(imp, axis=0))
    mm = mmat_ref[...]
    p_slc = _dot(i_hi, mm) + _dot(i_mid, mm) + _dot(i_lo, mm)

    kvw = [win_ref[0, g, pl.ds(pl.multiple_of(s0, NSA_Q_BLOCK), WIN_KEYS), :] for g in groups]
    s_w = [(_dot_nt(qg2[g], kvw[g]) + bwin_ref[g] + front_w).astype(BF16) for g in groups]
    m_w = [jnp.max(s_w[g], axis=1, keepdims=True) for g in groups]
    p_w = [jnp.exp(s_w[g] - m_w[g]) for g in groups]
    l_w = [jnp.sum(p_w[g].astype(F32), axis=1, keepdims=True) for g in groups]
    o_w = [_dot(p_w[g], kvw[g])[:, HEAD_DIM:] / l_w[g] for g in groups]

    nq2 = NSA_GROUPS * NSA_Q_BLOCK
    p_t = p_slc.T
    blk = lax.broadcasted_iota(jnp.int32, (nb, nq2), 0)
    cand = (blk >= 1) & (blk <= blk_t - 2)
    c = jnp.where(cand, p_t, -jnp.inf)
    sel_t = jnp.zeros((nb, nq2), F32)
    for _ in range(SEL_TOP_N - N_FORCED):
        mx = jnp.max(c, axis=0, keepdims=True)
        first = jnp.min(jnp.where(c == mx, blk, nb), axis=0, keepdims=True)
        hit = blk == first
        sel_t = jnp.where(hit, 1.0, sel_t)
        c = jnp.where(hit, -jnp.inf, c)
    forced = (blk == 0) | (blk == blk_t) | (blk == blk_t - 1)
    all_causal = (blk <= blk_t) & (blk_t < SEL_TOP_N)
    sel_t = jnp.where(forced | all_causal, 1.0, sel_t)
    neg_tail_t = jnp.where(sel_t > 0, 0.0, NEG)
    neg_far_t = jnp.where(blk < blk_t - 2, neg_tail_t, NEG)
    neg_tail = neg_tail_t.T.astype(BF16)
    neg_far = neg_far_t.T.astype(BF16)
    q_far, q_tail = [], []
    for g in groups:
        r = slice(g * NSA_Q_BLOCK, (g + 1) * NSA_Q_BLOCK)
        q_far.append(jnp.concatenate([qg2[g], stack_heads(neg_far[r])], axis=1))
        q_tail.append(jnp.concatenate([qg2[g], stack_heads(neg_tail[r])], axis=1))

    def attend(carry, q_aug, start, n_keys, bias):
        kv = [slc_ref[0, g, pl.ds(start, n_keys), :] for g in groups]
        s = [_dot_nt(q_aug[g], kv[g]) for g in groups]
        if bias is not None:
            s = [s[g] + bias[g] for g in groups]
        s = [s[g].astype(BF16) for g in groups]
        m_new = [jnp.maximum(carry[g][0], jnp.max(s[g], axis=1, keepdims=True).astype(F32)) for g in groups]
        p = [jnp.exp(s[g] - m_new[g].astype(BF16)) for g in groups]
        pv = [_dot(p[g], kv[g]) for g in groups]
        return tuple((m_new[g], jnp.exp(carry[g][0] - m_new[g]) * carry[g][1] + pv[g]) for g in groups)

    n_far = (jnp.maximum(blk_t - 2, 0) + FAR_TILE_BLOCKS - 1) // FAR_TILE_BLOCKS

    def far_body(j, carry):
        return attend(carry, q_far, pl.multiple_of(SEL_NEAR + j * FAR_TILE, LANES), FAR_TILE, None)

    init = (jnp.full((rows, 1), NEG, BF16).astype(F32), jnp.zeros((rows, 2 * HEAD_DIM + nb), F32))
    carry = lax.fori_loop(0, n_far, far_body, (init,) * NSA_GROUPS)
    carry = attend(carry, q_tail, pl.multiple_of(s0, NSA_Q_BLOCK), TAIL_KEYS,
                   [btail_ref[var, g] for g in groups])

    for g in groups:
        acc = carry[g][1]
        o_s = acc[:, HEAD_DIM:2 * HEAD_DIM] / jnp.sum(acc[:, 2 * HEAD_DIM:], axis=1, keepdims=True)
        for h in range(HEADS_PER_GROUP):
            head = g * HEADS_PER_GROUP + h
            r = slice(h * NSA_Q_BLOCK, (h + 1) * NSA_Q_BLOCK)
            o = (gates[:, head:head + 1] * o_c[g][r]
                 + gates[:, NSA_HEADS + head:NSA_HEADS + head + 1] * o_s[r]
                 + gates[:, 2 * NSA_HEADS + head:2 * NSA_HEADS + head + 1] * o_w[g][r])
            o_ref[0, :, head * HEAD_DIM:(head + 1) * HEAD_DIM] = o.astype(BF16)


def _nsa(qn, gn, kvcmp, slc, win, mmat, btail, bwin):
    B, S, _ = qn.shape
    G = NSA_GROUPS
    nq = S // NSA_Q_BLOCK
    qb = NSA_Q_BLOCK
    return pl.pallas_call(
        _nsa_kernel,
        grid=(B, nq),
        in_specs=[pl.BlockSpec((1, qb, 512), lambda b, i: (b, i, 0)),
                  pl.BlockSpec((1, qb, LANES), lambda b, i: (b, i, 0)),
                  pl.BlockSpec((1, G) + kvcmp.shape[2:], lambda b, i: (b, 0, 0, 0)),
                  pl.BlockSpec((1, G) + slc.shape[2:], lambda b, i: (b, 0, 0, 0)),
                  pl.BlockSpec((1, G) + win.shape[2:], lambda b, i: (b, 0, 0, 0)),
                  pl.BlockSpec(mmat.shape, lambda b, i: (0, 0)),
                  pl.BlockSpec(btail.shape, lambda b, i: (0, 0, 0, 0)),
                  pl.BlockSpec(bwin.shape, lambda b, i: (0, 0, 0))],
        out_specs=pl.BlockSpec((1, qb, 512), lambda b, i: (b, i, 0)),
        out_shape=jax.ShapeDtypeStruct((B, S, 512), BF16),
        compiler_params=pltpu.CompilerParams(dimension_semantics=("arbitrary", "arbitrary"),
                                             vmem_limit_bytes=VMEM_LIMIT),
    )(qn, gn, kvcmp, slc, win, mmat, btail, bwin)


def _swa_kernel(qs_ref, kv_ref, bias_ref, sink_ref, o_ref):
    qi = pl.program_id(1)
    s0 = qi * SWA_Q_BLOCK
    hg = SWA_HEADS // SWA_KV_HEADS
    q = qs_ref[0]
    var = jnp.minimum(qi, 1)
    groups = range(SWA_KV_HEADS)

    def group_rows(g):
        qg = jnp.concatenate([q[:, (g * hg + h) * HEAD_DIM:(g * hg + h + 1) * HEAD_DIM] for h in range(hg)],
                             axis=0)
        return jnp.concatenate([qg, jnp.zeros_like(qg)], axis=1)

    qg2 = [group_rows(g) for g in groups]
    kv = [kv_ref[0, g, pl.ds(pl.multiple_of(s0, SWA_Q_BLOCK), SWA_KEYS), :] for g in groups]
    s = [_dot_nt(qg2[g], kv[g]) + bias_ref[var, g] for g in groups]
    sink = [sink_ref[g] for g in groups]
    m = [jnp.maximum(jnp.max(s[g], axis=1, keepdims=True), sink[g]) for g in groups]
    p = [jnp.exp(s[g] - m[g]) for g in groups]
    l = [jnp.sum(p[g], axis=1, keepdims=True) + jnp.exp(sink[g] - m[g]) for g in groups]
    o = [_dot(p[g].astype(BF16), kv[g])[:, HEAD_DIM:] / l[g] for g in groups]
    for g in groups:
        for h in range(hg):
            head = g * hg + h
            o_ref[0, :, head * HEAD_DIM:(head + 1) * HEAD_DIM] = (
                o[g][h * SWA_Q_BLOCK:(h + 1) * SWA_Q_BLOCK].astype(BF16))


def _swa(qs, kvp, bias, sink_rows):
    B, S, _ = qs.shape
    qb = SWA_Q_BLOCK
    return pl.pallas_call(
        _swa_kernel,
        grid=(B, S // qb),
        in_specs=[pl.BlockSpec((1, qb, 512), lambda b, i: (b, i, 0)),
                  pl.BlockSpec((1, SWA_KV_HEADS) + kvp.shape[2:], lambda b, i: (b, 0, 0, 0)),
                  pl.BlockSpec(bias.shape, lambda b, i: (0, 0, 0, 0)),
                  pl.BlockSpec(sink_rows.shape, lambda b, i: (0, 0, 0))],
        out_specs=pl.BlockSpec((1, qb, 512), lambda b, i: (b, i, 0)),
        out_shape=jax.ShapeDtypeStruct((B, S, 512), BF16),
        compiler_params=pltpu.CompilerParams(dimension_semantics=("arbitrary", "arbitrary"),
                                             vmem_limit_bytes=VMEM_LIMIT),
    )(qs, kvp, bias, sink_rows)


def _pack_bf16_pair(lo, hi):
    lo_b = pltpu.bitcast(lo.astype(BF16).astype(F32), jnp.uint32)
    hi_b = pltpu.bitcast(hi.astype(BF16).astype(F32), jnp.uint32)
    return (lo_b >> 16) | (hi_b & jnp.uint32(0xFFFF0000))


def _unpack_bf16_pair(w):
    lo = pltpu.bitcast(w << 16, F32).astype(BF16)
    hi = pltpu.bitcast(w & jnp.uint32(0xFFFF0000), F32).astype(BF16)
    return lo, hi


def _merge_kernel(on_ref, os_ref, gm_ref, x_ref, mod_ref, wb_ref, wo_ref, g_ref, wr_ref, br_ref, ltri_ref,
                  xo_ref, hp_ref, idx_ref, gate_ref, cnt_ref, run_ref):
    d = D_MODEL
    a = _dot(on_ref[...], wb_ref[0])
    b = _dot(os_ref[...], wb_ref[1])
    gm = gm_ref[...]
    merged = _sigmoid(gm[:, :d]) * a + _sigmoid(gm[:, d:]) * b
    y = _dot(merged.astype(BF16), wo_ref[...])
    xn = x_ref[...] + mod_ref[0, 2:3, :] * y
    xo_ref[...] = xn
    ms = jnp.mean(xn * xn, axis=-1, keepdims=True)
    h2 = xn * lax.rsqrt(ms + RMS_EPS) * g_ref[...]
    h2 = h2 * (1.0 + mod_ref[0, 4:5, :]) + mod_ref[0, 3:4, :]
    hp_ref[...] = _pack_bf16_pair(h2[:, :d // 2], h2[:, d // 2:])
    h_hi, h_mid, h_lo = _split3(h2)
    w_hi, w_mid, w_lo = wr_ref[0], wr_ref[1], wr_ref[2]
    logits = (_dot(h_hi, w_hi) + (_dot(h_hi, w_mid) + _dot(h_mid, w_hi))
              + (_dot(h_hi, w_lo) + _dot(h_mid, w_mid) + _dot(h_lo, w_hi))) + br_ref[...]
    tm = logits.shape[0]
    lane = lax.broadcasted_iota(jnp.int32, (tm, LANES), 1)
    c = jnp.where(lane < N_EXPERTS, logits, -jnp.inf)
    vals, idxs = [], []
    for _ in range(TOP_K):
        mx = jnp.max(c, axis=1, keepdims=True)
        first = jnp.min(jnp.where(c == mx, lane, LANES), axis=1, keepdims=True)
        vals.append(mx)
        idxs.append(first)
        c = jnp.where(lane == first, -jnp.inf, c)
    es = [jnp.exp(v - vals[0]) for v in vals]
    den = es[0] + es[1] + es[2] + es[3]

    @pl.when(pl.program_id(0) == 0)
    def _():
        run_ref[...] = jnp.zeros_like(run_ref)

    onehot = jnp.zeros((tm, LANES), F32)
    for k in range(TOP_K):
        onehot = jnp.where(lane == idxs[k], 1.0, onehot)
    before = _dot(ltri_ref[...], onehot.astype(BF16)) + run_ref[...]
    run_ref[...] = run_ref[...] + jnp.sum(onehot, axis=0, keepdims=True)
    cnt_ref[...] = run_ref[...].astype(jnp.int32)

    idx_out = jnp.zeros((tm, LANES), jnp.int32)
    gate_out = jnp.zeros((tm, LANES), F32)
    for k in range(TOP_K):
        rank = jnp.sum(jnp.where(lane == idxs[k], before, 0.0), axis=1, keepdims=True).astype(jnp.int32)
        idx_out = jnp.where(lane == k, idxs[k], idx_out)
        idx_out = jnp.where(lane == TOP_K + k, rank, idx_out)
        gate_out = jnp.where(lane == k, es[k] / den, gate_out)
    idx_ref[...] = idx_out
    gate_ref[...] = gate_out


def _merge(on, os_, gm, x, mod, wb, wo, g_ffn, wr3, br, S):
    T, D = x.shape
    tm = 256
    nb_per_seq = S // tm
    ltri = jnp.asarray(np.tril(np.ones((tm, tm), np.float32), -1), BF16)
    return pl.pallas_call(
        _merge_kernel,
        grid=(T // tm,),
        in_specs=[pl.BlockSpec((tm, 512), lambda i: (i, 0)),
                  pl.BlockSpec((tm, 512), lambda i: (i, 0)),
                  pl.BlockSpec((tm, 2 * D), lambda i: (i, 0)),
                  pl.BlockSpec((tm, D), lambda i: (i, 0)),
                  pl.BlockSpec((1, 6, D), lambda i: (i // nb_per_seq, 0, 0)),
                  pl.BlockSpec((2, 512, D), lambda i: (0, 0, 0)),
                  pl.BlockSpec((D, D), lambda i: (0, 0)),
                  pl.BlockSpec((1, D), lambda i: (0, 0)),
                  pl.BlockSpec((3, D, LANES), lambda i: (0, 0, 0)),
                  pl.BlockSpec((1, LANES), lambda i: (0, 0)),
                  pl.BlockSpec((tm, tm), lambda i: (0, 0))],
        out_specs=[pl.BlockSpec((tm, D), lambda i: (i, 0)),
                   pl.BlockSpec((tm, D // 2), lambda i: (i, 0)),
                   pl.BlockSpec((tm, LANES), lambda i: (i, 0)),
                   pl.BlockSpec((tm, LANES), lambda i: (i, 0)),
                   pl.BlockSpec((1, LANES), lambda i: (0, 0))],
        out_shape=[jax.ShapeDtypeStruct((T, D), F32),
                   jax.ShapeDtypeStruct((T, D // 2), jnp.uint32),
                   jax.ShapeDtypeStruct((T, LANES), jnp.int32),
                   jax.ShapeDtypeStruct((T, LANES), F32),
                   jax.ShapeDtypeStruct((1, LANES), jnp.int32)],
        scratch_shapes=[pltpu.VMEM((1, LANES), F32)],
        compiler_params=pltpu.CompilerParams(dimension_semantics=("arbitrary",),
                                             vmem_limit_bytes=VMEM_LIMIT),
    )(on, os_, gm, x, mod, wb, wo, g_ffn, wr3, br, ltri)


def _dispatch_kernel(dest_ref, hp_ref, xs_in_hbm, xs_hbm, sem):
    del xs_in_hbm
    tm = hp_ref.shape[0]

    def body(r, _):
        for k in range(TOP_K):
            pltpu.make_async_copy(hp_ref.at[pl.ds(r, 1), :],
                                  xs_hbm.at[pl.ds(dest_ref[0, 0, r * TOP_K + k], 1), :], sem).start()
        return 0
    lax.fori_loop(0, tm, body, 0, unroll=DMA_LOOP_UNROLL)
    for _ in range(TOP_K):
        pltpu.make_async_copy(hp_ref, xs_hbm.at[pl.ds(0, tm), :], sem).wait()


def _dispatch(dest, hp, n_slots):
    T, W = hp.shape
    tm = 256
    xs0 = jnp.zeros((n_slots, W), hp.dtype)
    return pl.pallas_call(
        _dispatch_kernel,
        grid=(T // tm,),
        in_specs=[pl.BlockSpec((1, 1, tm * TOP_K), lambda i: (i, 0, 0), memory_space=pltpu.SMEM),
                  pl.BlockSpec((tm, W), lambda i: (i, 0)),
                  pl.BlockSpec(memory_space=pl.ANY)],
        out_specs=pl.BlockSpec(memory_space=pl.ANY),
        out_shape=jax.ShapeDtypeStruct((n_slots, W), hp.dtype),
        scratch_shapes=[pltpu.SemaphoreType.DMA(())],
        input_output_aliases={2: 0},
        compiler_params=pltpu.CompilerParams(dimension_semantics=("arbitrary",),
                                             vmem_limit_bytes=VMEM_LIMIT),
    )(dest.reshape(T // tm, 1, tm * TOP_K), hp, xs0)


def _moe_kernel(bexp_ref, used_ref, xs_ref, wu_ref, wd_ref, bu_ref, bd_ref, ys_ref):
    i = pl.program_id(0)

    @pl.when(used_ref[i] > 0)
    def _():
        x_lo, x_hi = _unpack_bf16_pair(xs_ref[...])
        half = D_MODEL // 2
        hid = []
        for t in range(2 * D_EXPERT // UP_TILE):
            cols = slice(t * UP_TILE, (t + 1) * UP_TILE)
            gu = _dot(x_lo, wu_ref[0, :half, cols]) + _dot(x_hi, wu_ref[0, half:, cols]) + bu_ref[0, :, cols]
            glu = jnp.minimum(gu[:, :UP_TILE // 2], SWIGLU_LIMIT)
            lin = jnp.clip(gu[:, UP_TILE // 2:], -SWIGLU_LIMIT, SWIGLU_LIMIT)
            hid.append((glu * _sigmoid(SWIGLU_ALPHA * glu) * (lin + 1.0)).astype(BF16))
        ys_ref[...] = _dot(jnp.concatenate(hid, axis=1), wd_ref[0]) + bd_ref[0]

    @pl.when(used_ref[i] == 0)
    def _():
        ys_ref[...] = jnp.zeros_like(ys_ref)


def _wup_prep_kernel(w_ref, p_ref, o_ref):
    for t in range(w_ref.shape[2] // UP_TILE):
        cols = slice(t * UP_TILE, (t + 1) * UP_TILE)
        o_ref[0, :, cols] = _dot(w_ref[0, :, cols].astype(BF16), p_ref[...]).astype(BF16)


def _wup_prep(w_up):
    N, D, W = w_up.shape
    j = np.arange(UP_TILE)
    perm = np.zeros((UP_TILE, UP_TILE), np.float32)
    perm[j, (j % 2) * (UP_TILE // 2) + j // 2] = 1.0
    return pl.pallas_call(
        _wup_prep_kernel,
        grid=(N,),
        in_specs=[pl.BlockSpec((1, D, W), lambda n: (n, 0, 0)),
                  pl.BlockSpec((UP_TILE, UP_TILE), lambda n: (0, 0))],
        out_specs=pl.BlockSpec((1, D, W), lambda n: (n, 0, 0)),
        out_shape=jax.ShapeDtypeStruct((N, D, W), BF16),
        compiler_params=pltpu.CompilerParams(dimension_semantics=("arbitrary",),
                                             vmem_limit_bytes=VMEM_LIMIT),
    )(w_up, jnp.asarray(perm, BF16))


def _moe(bexp, used, xs, wu, wd, bu, bd):
    nblk = bexp.shape[0]
    D = D_MODEL
    tm = MOE_BLOCK
    grid_spec = pltpu.PrefetchScalarGridSpec(
        num_scalar_prefetch=2,
        grid=(nblk,),
        in_specs=[pl.BlockSpec((tm, D // 2), lambda i, be, us: (i, 0)),
                  pl.BlockSpec((1, D, 2 * D_EXPERT), lambda i, be, us: (be[i], 0, 0)),
                  pl.BlockSpec((1, D_EXPERT, D), lambda i, be, us: (be[i], 0, 0)),
                  pl.BlockSpec((1, 1, 2 * D_EXPERT), lambda i, be, us: (be[i], 0, 0)),
                  pl.BlockSpec((1, 1, D), lambda i, be, us: (be[i], 0, 0))],
        out_specs=pl.BlockSpec((tm, D), lambda i, be, us: (i, 0)),
    )
    return pl.pallas_call(
        _moe_kernel,
        grid_spec=grid_spec,
        out_shape=jax.ShapeDtypeStruct((nblk * tm, D), F32),
        compiler_params=pltpu.CompilerParams(dimension_semantics=("arbitrary",),
                                             vmem_limit_bytes=VMEM_LIMIT),
    )(bexp, used, xs, wu, wd, bu, bd)


def _combine_kernel(dest_ref, destn_ref, x_ref, gate_ref, mod_ref, g_ref, ys_hbm, o_ref, ybuf, sem, *, final):
    i = pl.program_id(0)
    nt = pl.num_programs(0)
    slot = i % 2
    tm = x_ref.shape[0]

    def issue(ids_ref, sl):
        def body(r, _):
            for k in range(TOP_K):
                pltpu.make_async_copy(ys_hbm.at[pl.ds(ids_ref[0, 0, r * TOP_K + k], 1), :],
                                      ybuf.at[sl, k, pl.ds(r, 1), :], sem.at[sl]).start()
            return 0
        lax.fori_loop(0, tm, body, 0, unroll=DMA_LOOP_UNROLL)

    @pl.when(i == 0)
    def _():
        issue(dest_ref, 0)

    @pl.when(i + 1 < nt)
    def _():
        issue(destn_ref, 1 - slot)

    for k in range(TOP_K):
        pltpu.make_async_copy(ys_hbm.at[pl.ds(0, tm), :], ybuf.at[slot, k], sem.at[slot]).wait()

    gate = gate_ref[...]
    y = gate[:, 0:1] * ybuf[slot, 0]
    for k in range(1, TOP_K):
        y = y + gate[:, k:k + 1] * ybuf[slot, k]
    xn = x_ref[...] + mod_ref[0, 5:6, :] * y
    if final:
        ms = jnp.mean(xn * xn, axis=-1, keepdims=True)
        xn = xn * lax.rsqrt(ms + RMS_EPS) * g_ref[...]
    o_ref[...] = xn


def _combine(dest, x, ys, gate, mod, g_final, S, final):
    T, D = x.shape
    tm = 256
    nb_per_seq = S // tm
    nt = T // tm
    dest3 = dest.reshape(nt, 1, tm * TOP_K)
    smem_blk = lambda f: pl.BlockSpec((1, 1, tm * TOP_K), f, memory_space=pltpu.SMEM)
    return pl.pallas_call(
        functools.partial(_combine_kernel, final=final),
        grid=(nt,),
        in_specs=[smem_blk(lambda i: (i, 0, 0)),
                  smem_blk(lambda i: (jnp.minimum(i + 1, nt - 1), 0, 0)),
                  pl.BlockSpec((tm, D), lambda i: (i, 0)),
                  pl.BlockSpec((tm, LANES), lambda i: (i, 0)),
                  pl.BlockSpec((1, 6, D), lambda i: (i // nb_per_seq, 0, 0)),
                  pl.BlockSpec((1, D), lambda i: (0, 0)),
                  pl.BlockSpec(memory_space=pl.ANY)],
        out_specs=pl.BlockSpec((tm, D), lambda i: (i, 0)),
        out_shape=jax.ShapeDtypeStruct((T, D), F32),
        scratch_shapes=[pltpu.VMEM((2, TOP_K, tm, D), F32),
                        pltpu.SemaphoreType.DMA((2,))],
        compiler_params=pltpu.CompilerParams(dimension_semantics=("arbitrary",),
                                             vmem_limit_bytes=VMEM_LIMIT),
    )(dest3, dest3, x, gate, mod, g_final, ys)


def _t5_bucket(dist):
    n = jnp.maximum(dist, 0)
    max_exact = REL_BUCKETS // 2
    log_ratio = jnp.log(jnp.maximum(n, 1).astype(F32) / max_exact) / math.log(REL_MAX_DIST / max_exact)
    large = max_exact + (log_ratio * (REL_BUCKETS - max_exact)).astype(jnp.int32)
    return jnp.where(n < max_exact, n, jnp.minimum(large, REL_BUCKETS - 1))


def _toeplitz_bias(tab, q_block, n_keys, back, window, variants, shift_far):
    r = jnp.arange(q_block)[:, None]
    c = jnp.arange(n_keys)[None, :]
    dist = r + back - c
    ok = dist >= 0
    if window is not None:
        ok = ok & (dist < window)
    bias = tab[_t5_bucket(dist)]
    if shift_far:
        bias = bias - tab[REL_BUCKETS - 1][None, None]
    bias = jnp.transpose(bias, (2, 3, 0, 1))
    outs = []
    for v in range(variants):
        okv = ok if v == variants - 1 else ok & (c >= back - v * q_block)
        outs.append(jnp.where(okv[None, None], bias, NEG))
    out = jnp.stack(outs)
    V, G, Hg = out.shape[:3]
    return out.reshape(V, G, Hg * q_block, n_keys).astype(F32)


def _front_pad_groups(kv, pad):
    B, S, _ = kv.shape
    kv = kv.reshape(B, S, NSA_GROUPS, 2 * HEAD_DIM).transpose(0, 2, 1, 3)
    return jnp.pad(kv, ((0, 0), (0, 0), (pad, 0), (0, 0)))


def _routing_tables(route, counts, T):
    E = N_EXPERTS
    n_blocks = -(-(T * TOP_K) // MOE_BLOCK) + E
    padded = (counts + MOE_BLOCK - 1) // MOE_BLOCK * MOE_BLOCK
    pad_end = jnp.cumsum(padded)
    grp_start = pad_end - padded
    e_idx = route[:, :TOP_K]
    onehot = e_idx[..., None] == jnp.arange(E, dtype=jnp.int32)
    dest = jnp.sum(jnp.where(onehot, grp_start, 0), axis=-1) + route[:, TOP_K:2 * TOP_K]
    blk_start = jnp.arange(n_blocks, dtype=jnp.int32) * MOE_BLOCK
    bexp = jnp.minimum(jnp.sum(pad_end[None, :] <= blk_start[:, None], axis=1), E - 1).astype(jnp.int32)
    used = (blk_start < pad_end[-1]).astype(jnp.int32)
    return dest.astype(jnp.int32), bexp, used


def kernel(x, c, w_ada, b_ada, g_mix, g_ffn, g_final, w_in, cmp_pos, cmp_w1, cmp_w2, sinks, rel_tab,
           w_branch, w_out, w_router, b_router, w_up, b_up, w_down, b_down):
    B, S, D = x.shape
    L = w_ada.shape[0]
    T = B * S
    G = NSA_GROUPS
    nb = S // SEL_BLOCK
    ncp = S // CMP_STRIDE

    perm = _in_perm()
    w_in_p = jnp.take(jnp.pad(w_in, ((0, 0), (0, 0), (0, 1))), perm, axis=2).astype(BF16)
    pos_r = cmp_pos.reshape(L, 2, 2, 1, CMP_STRIDE * HEAD_DIM)
    w1_r = cmp_w1.reshape(L, 2, 2, CMP_STRIDE * HEAD_DIM, CMP_HIDDEN).astype(BF16)
    w2_r = cmp_w2.astype(BF16)
    wb = w_branch.astype(BF16)
    wo = w_out.astype(BF16)
    wr_pad = jnp.pad(w_router, ((0, 0), (0, 0), (0, LANES - N_EXPERTS)))
    wr_hi = wr_pad.astype(BF16)
    wr_r1 = wr_pad - wr_hi.astype(F32)
    wr_mid = wr_r1.astype(BF16)
    wr_lo = (wr_r1 - wr_mid.astype(F32)).astype(BF16)
    wr3 = jnp.stack([wr_hi, wr_mid, wr_lo], axis=1)
    br = jnp.pad(b_router, ((0, 0), (0, LANES - N_EXPERTS))).reshape(L, 1, LANES)
    wu = _wup_prep(w_up.reshape(L * N_EXPERTS, D, 2 * D_EXPERT)).reshape(L, N_EXPERTS, D, 2 * D_EXPERT)
    n_up_tiles = 2 * D_EXPERT // UP_TILE
    bu = b_up.reshape(L, N_EXPERTS, n_up_tiles, UP_TILE // 2, 2).transpose(0, 1, 2, 4, 3).reshape(
        L, N_EXPERTS, 1, 2 * D_EXPERT)
    wd = w_down.astype(BF16)
    bd = b_down.reshape(L, N_EXPERTS, 1, D)

    nsa_tab = rel_tab[:, :NSA_HEADS].reshape(REL_BUCKETS, G, HEADS_PER_GROUP)
    swa_tab = rel_tab[:, NSA_HEADS:].reshape(REL_BUCKETS, SWA_KV_HEADS, SWA_HEADS // SWA_KV_HEADS)
    btail = _toeplitz_bias(nsa_tab, NSA_Q_BLOCK, TAIL_KEYS, SEL_NEAR, None, 3, True)
    bwin = _toeplitz_bias(nsa_tab, NSA_Q_BLOCK, WIN_KEYS, NSA_WINDOW, NSA_WINDOW, 1, False)[0]
    bswa = _toeplitz_bias(swa_tab, SWA_Q_BLOCK, SWA_KEYS, SWA_WINDOW, SWA_WINDOW, 2, False)
    ci = jnp.arange(ncp)[:, None]
    bj = jnp.arange(nb)[None, :]
    delta = (SEL_BLOCK // CMP_STRIDE) * bj - ci
    mmat = jnp.where((delta == 0) | (delta == 4), 1.0, jnp.where((delta >= 1) & (delta <= 3), 2.0, 0.0))
    mmat = jnp.where(ci < ncp - 1, mmat, 0.0).astype(BF16)
    onehot_blk = (jnp.arange(S)[:, None] // SEL_BLOCK == jnp.arange(nb)[None, :]).astype(BF16)

    c_pad = jnp.pad(c, ((0, 8 - B), (0, 0)))
    mod_all = _ada_mod(c_pad, w_ada, b_ada)[:, :B].reshape(L, B, 6, D)

    xf = x
    for l in range(L):
        mod = mod_all[l]
        qn, kvc, kvsl, kvw, gn, qs, kvs, gm = _inproj(xf, mod, g_mix[l].reshape(1, D), w_in_p[l])
        xc = kvc.reshape(B, S, 2 * G, HEAD_DIM).transpose(0, 2, 1, 3).reshape(
            B, 2 * G, ncp, CMP_STRIDE * HEAD_DIM)
        kvcmp = _compress(xc, pos_r[l], w1_r[l], w2_r[l])
        slc = kvsl.reshape(B, S, G, 2 * HEAD_DIM).transpose(0, 2, 1, 3)
        slc = jnp.concatenate([slc, jnp.broadcast_to(onehot_blk, (B, G, S, nb))], axis=-1)
        slc = jnp.pad(slc, ((0, 0), (0, 0), (SEL_NEAR, 0), (0, 0)))
        win = _front_pad_groups(kvw, NSA_WINDOW)
        o_n = _nsa(qn, gn, kvcmp, slc, win, mmat, btail, bwin)
        swa_kv = _front_pad_groups(kvs, SWA_WINDOW)
        sink_rows = jnp.repeat(sinks[l].reshape(SWA_KV_HEADS, SWA_HEADS // SWA_KV_HEADS), SWA_Q_BLOCK,
                               axis=1)[..., None]
        o_s = _swa(qs, swa_kv, bswa, sink_rows)
        x2, hp, route, gate, counts = _merge(o_n.reshape(T, 512), o_s.reshape(T, 512), gm.reshape(T, 2 * D),
                                             xf.reshape(T, D), mod, wb[l], wo[l], g_ffn[l].reshape(1, D),
                                             wr3[l], br[l], S)
        dest, bexp, used = _routing_tables(route, counts[0, :N_EXPERTS], T)
        xs = _dispatch(dest, hp, bexp.shape[0] * MOE_BLOCK)
        ys = _moe(bexp, used, xs, wu[l], wd[l], bu[l], bd[l])
        xf = _combine(dest, x2, ys, gate, mod, g_final.reshape(1, D), S,
                      final=(l == L - 1)).reshape(B, S, D)
    return xf
```

```python
import functools
import math

import numpy as np
import jax
import jax.numpy as jnp
from jax import lax
from jax.experimental import pallas as pl
from jax.experimental.pallas import tpu as pltpu

D_MODEL = 1024
HEAD_DIM = 64
NSA_HEADS = 8
NSA_GROUPS = 2
HEADS_PER_GROUP = NSA_HEADS // NSA_GROUPS
CMP_LEN = 32
CMP_STRIDE = 16
CMP_HIDDEN = 256
SEL_BLOCK = 64
SEL_TOP_N = 16
NSA_WINDOW = 512
NSA_Q_BLOCK = 64
SWA_HEADS = 8
SWA_KV_HEADS = 2
SWA_WINDOW = 128
SWA_Q_BLOCK = 128
REL_BUCKETS = 32
REL_MAX_DIST = 128
N_EXPERTS = 32
TOP_K = 4
D_EXPERT = 1024
SWIGLU_ALPHA = 1.702
SWIGLU_LIMIT = 7.0
MOE_BLOCK = 256
RMS_EPS = 1e-5

LANES = 128
NEG = -1e30
FAR_TILE = 1024
FAR_TILE_BLOCKS = FAR_TILE // SEL_BLOCK
SEL_NEAR = 2 * SEL_BLOCK
TAIL_KEYS = SEL_NEAR + NSA_Q_BLOCK
WIN_KEYS = NSA_WINDOW + NSA_Q_BLOCK
SWA_KEYS = SWA_WINDOW + SWA_Q_BLOCK
KV_FRONT = 512
N_FORCED = 3
UP_TILE = 256
DMA_LOOP_UNROLL = 8
VMEM_LIMIT = 56 * 1024 * 1024

F32 = jnp.float32
BF16 = jnp.bfloat16

_OFF_QN = 0
_OFF_KVN = 512
_OFF_GN = 1280
_OFF_QS = 1304
_OFF_KVS = 1816
_OFF_GM = 2072
_D_IN = 4120
_P_QN, _P_CMP, _P_SLC, _P_WIN, _P_GN, _P_QS, _P_SWA, _P_GM, _P_END = (
    0, 512, 768, 1024, 1280, 1408, 1920, 2176, 4224)


def _sigmoid(v):
    return 1.0 / (1.0 + jnp.exp(-v))


def _dot(a, b):
    return jnp.dot(a, b, preferred_element_type=F32)


def _dot_nt(a, b):
    return lax.dot_general(a, b, (((1,), (1,)), ((), ())), preferred_element_type=F32)


def _split3(v):
    hi = v.astype(BF16)
    r1 = v - hi.astype(F32)
    mid = r1.astype(BF16)
    lo = (r1 - mid.astype(F32)).astype(BF16)
    return hi, mid, lo


def _reorder_in_proj(w_in):
    def cols(a, n):
        return w_in[..., a:a + n]

    def kv_pairs(base):
        return [cols(base + off, 64) for off in (0, 128, 64, 192)]

    parts = [cols(_OFF_QN, 512)]
    parts += kv_pairs(_OFF_KVN)
    parts += kv_pairs(_OFF_KVN + 256)
    parts += kv_pairs(_OFF_KVN + 512)
    parts += [cols(_OFF_GN, 24), jnp.zeros(w_in.shape[:-1] + (LANES - 24,), w_in.dtype)]
    parts += [cols(_OFF_QS, 512)]
    parts += kv_pairs(_OFF_KVS)
    parts += [cols(_OFF_GM, 2048)]
    out = jnp.concatenate(parts, axis=-1).astype(BF16)
    assert out.shape[-1] == _P_END
    return out


def _ada_kernel(c_ref, w_ref, b_ref, o_ref):
    c = c_ref[...]
    cond = c * _sigmoid(c)
    o_ref[0] = _dot(cond.astype(BF16), w_ref[0].astype(BF16)) + b_ref[0]


def _ada_mod(c_pad, w_ada, b_ada):
    L, D, N = w_ada.shape
    tn = 1536
    return pl.pallas_call(
        _ada_kernel,
        grid=(L, N // tn),
        in_specs=[pl.BlockSpec((8, D), lambda l, j: (0, 0)),
                  pl.BlockSpec((1, D, tn), lambda l, j: (l, 0, j)),
                  pl.BlockSpec((1, 1, tn), lambda l, j: (l, 0, j))],
        out_specs=pl.BlockSpec((1, 8, tn), lambda l, j: (l, 0, j)),
        out_shape=jax.ShapeDtypeStruct((L, 8, N), F32),
        compiler_params=pltpu.CompilerParams(dimension_semantics=("arbitrary", "arbitrary"),
                                             vmem_limit_bytes=VMEM_LIMIT),
    )(c_pad, w_ada, b_ada.reshape(L, 1, N))


def _inproj_kernel(x_ref, mod_ref, g_ref, w_ref, slc_in, win_in, swa_in,
                   qn_ref, kvc_ref, slc_ref, win_ref, gn_ref, qs_ref, swa_ref, gm_ref):
    del slc_in, win_in, swa_in
    x = x_ref[0]
    ms = jnp.mean(x * x, axis=-1, keepdims=True)
    y = x * lax.rsqrt(ms + RMS_EPS) * g_ref[...]
    h = (y * (1.0 + mod_ref[0, 1:2, :]) + mod_ref[0, 0:1, :]).astype(BF16)

    def proj(a, b):
        return _dot(h, w_ref[:, a:b])

    def per_group(ref, vals, width):
        for c in range(vals.shape[1] // width):
            ref[0, c] = vals[:, c * width:(c + 1) * width].astype(ref.dtype)

    scale = HEAD_DIM ** -0.5
    qn_ref[0] = (proj(_P_QN, _P_CMP) * scale).astype(BF16)
    per_group(kvc_ref, proj(_P_CMP, _P_SLC), HEAD_DIM)
    per_group(slc_ref, proj(_P_SLC, _P_WIN), 2 * HEAD_DIM)
    per_group(win_ref, proj(_P_WIN, _P_GN), 2 * HEAD_DIM)
    gn_ref[0] = proj(_P_GN, _P_QS)
    qs_ref[0] = (proj(_P_QS, _P_SWA) * scale).astype(BF16)
    per_group(swa_ref, proj(_P_SWA, _P_GM), 2 * HEAD_DIM)
    gm_ref[0] = proj(_P_GM, _P_END)


def _inproj(x, mod, g, w, slc_t, win_t, swa_t):
    B, S, D = x.shape
    G = NSA_GROUPS
    tm = KV_FRONT
    front = KV_FRONT // tm
    plain = lambda n: pl.BlockSpec((1, tm, n), lambda b, i: (b, i, 0))
    padded = pl.BlockSpec((1, G, tm, 2 * HEAD_DIM), lambda b, i: (b, 0, i + front, 0))
    hbm = pl.BlockSpec(memory_space=pl.ANY)
    sds = jax.ShapeDtypeStruct
    return pl.pallas_call(
        _inproj_kernel,
        grid=(B, S // tm),
        in_specs=[pl.BlockSpec((1, tm, D), lambda b, i: (b, i, 0)),
                  pl.BlockSpec((1, 6, D), lambda b, i: (b, 0, 0)),
                  pl.BlockSpec((1, D), lambda b, i: (0, 0)),
                  pl.BlockSpec((D, _P_END), lambda b, i: (0, 0)),
                  hbm, hbm, hbm],
        out_specs=[plain(512),
                   pl.BlockSpec((1, 2 * G, tm, HEAD_DIM), lambda b, i: (b, 0, i, 0)),
                   padded, padded, plain(LANES), plain(512), padded, plain(2 * D)],
        out_shape=[sds((B, S, 512), BF16), sds((B, 2 * G, S, HEAD_DIM), F32),
                   sds(slc_t.shape, BF16), sds(win_t.shape, BF16), sds((B, S, LANES), F32),
                   sds((B, S, 512), BF16), sds(swa_t.shape, BF16), sds((B, S, 2 * D), F32)],
        input_output_aliases={4: 2, 5: 3, 6: 6},
        compiler_params=pltpu.CompilerParams(dimension_semantics=("arbitrary", "arbitrary"),
                                             vmem_limit_bytes=VMEM_LIMIT),
    )(x, mod, g, w, slc_t, win_t, swa_t)


def _gelu_tanh(v):
    return 0.5 * v * (1.0 + jnp.tanh(math.sqrt(2.0 / math.pi) * (v + 0.044715 * (v * v * v))))


def _compress_kernel(x_ref, pos_ref, w1_ref, w2_ref, o_ref):
    nch = x_ref.shape[2]
    row = lax.broadcasted_iota(jnp.int32, (nch, 1), 0)
    outs = []
    for kv in range(2):
        xc = x_ref[0, kv]
        a = _dot((xc + pos_ref[kv, 0]).astype(BF16), w1_ref[kv, 0])
        b = _dot((xc + pos_ref[kv, 1]).astype(BF16), w1_ref[kv, 1])
        b_next = pltpu.roll(b, nch - 1, axis=0)
        pre = a + jnp.where(row < nch - 1, b_next, 0.0)
        outs.append(_dot(_gelu_tanh(pre).astype(BF16), w2_ref[kv]))
    o_ref[0, 0] = jnp.concatenate(outs, axis=-1).astype(BF16)


def _compress(xc, pos, w1, w2):
    B, _, nch, cw = xc.shape
    return pl.pallas_call(
        _compress_kernel,
        grid=(B, NSA_GROUPS),
        in_specs=[pl.BlockSpec((1, 2, nch, cw), lambda b, g: (b, g, 0, 0)),
                  pl.BlockSpec((2, 2, 1, cw), lambda b, g: (0, 0, 0, 0)),
                  pl.BlockSpec((2, 2, cw, CMP_HIDDEN), lambda b, g: (0, 0, 0, 0)),
                  pl.BlockSpec((2, CMP_HIDDEN, HEAD_DIM), lambda b, g: (0, 0, 0))],
        out_specs=pl.BlockSpec((1, 1, nch, 2 * HEAD_DIM), lambda b, g: (b, g, 0, 0)),
        out_shape=jax.ShapeDtypeStruct((B, NSA_GROUPS, nch, 2 * HEAD_DIM), BF16),
        compiler_params=pltpu.CompilerParams(dimension_semantics=("arbitrary", "arbitrary"),
                                             vmem_limit_bytes=VMEM_LIMIT),
    )(xc, pos, w1, w2)


def _nsa_kernel(qn_ref, gn_ref, cmp_ref, slc_ref, win_ref, mmat_ref, btail_ref, bwin_ref, o_ref,
                sa_ref, sb_ref):
    qi = pl.program_id(1)
    s0 = qi * NSA_Q_BLOCK
    blk_t = qi
    ncp = cmp_ref.shape[2]
    nb = mmat_ref.shape[1]
    rows = HEADS_PER_GROUP * NSA_Q_BLOCK
    q = qn_ref[0]
    gates = _sigmoid(gn_ref[0])
    var = jnp.minimum(qi, 2)

    def stack_heads(v):
        return jnp.concatenate([v] * HEADS_PER_GROUP, axis=0)

    groups = range(NSA_GROUPS)
    col_c = lax.broadcasted_iota(jnp.int32, (NSA_Q_BLOCK, ncp), 1)
    t_c = s0 + lax.broadcasted_iota(jnp.int32, (NSA_Q_BLOCK, ncp), 0)
    bias_c = stack_heads(jnp.where(col_c * CMP_STRIDE + (CMP_LEN - 1) <= t_c, 0.0, NEG))
    col_w = lax.broadcasted_iota(jnp.int32, (1, WIN_KEYS), 1)
    front_w = jnp.where(col_w >= NSA_WINDOW - s0, 0.0, NEG)
    def group_rows(g):
        qg = jnp.concatenate(
            [q[:, (g * HEADS_PER_GROUP + h) * HEAD_DIM:(g * HEADS_PER_GROUP + h + 1) * HEAD_DIM]
             for h in range(HEADS_PER_GROUP)], axis=0)
        return jnp.concatenate([qg, jnp.zeros_like(qg)], axis=1)

    qg2 = [group_rows(g) for g in groups]

    kvc = [cmp_ref[0, g] for g in groups]
    s_c = [_dot_nt(qg2[g], kvc[g]) + bias_c for g in groups]
    m_c = [jnp.max(s_c[g], axis=1, keepdims=True) for g in groups]
    e_c = [jnp.exp(s_c[g] - m_c[g]) for g in groups]
    d_c = [jnp.sum(e_c[g], axis=1, keepdims=True) for g in groups]
    p_c = [e_c[g] * jnp.where(m_c[g] > 0.5 * NEG, 1.0 / d_c[g], 0.0) for g in groups]
    o_c = [_dot(p_c[g].astype(BF16), kvc[g])[:, HEAD_DIM:] for g in groups]
    imp = [(p_c[g][0:NSA_Q_BLOCK] + p_c[g][NSA_Q_BLOCK:2 * NSA_Q_BLOCK]
            + p_c[g][2 * NSA_Q_BLOCK:3 * NSA_Q_BLOCK] + p_c[g][3 * NSA_Q_BLOCK:]) for g in groups]
    i_hi, i_mid, i_lo = _split3(jnp.concatenate(imp, axis=0))
    mm = mmat_ref[...]
    p_slc = _dot(i_hi, mm) + _dot(i_mid, mm) + _dot(i_lo, mm)

    win0 = pl.multiple_of(KV_FRONT - NSA_WINDOW + s0, NSA_Q_BLOCK)
    kvw = [win_ref[0, g, pl.ds(win0, WIN_KEYS), :] for g in groups]
    s_w = [(_dot_nt(qg2[g], kvw[g]) + bwin_ref[g] + front_w).astype(BF16) for g in groups]
    m_w = [jnp.max(s_w[g], axis=1, keepdims=True) for g in groups]
    p_w = [jnp.exp(s_w[g] - m_w[g]) for g in groups]
    l_w = [jnp.sum(p_w[g].astype(F32), axis=1, keepdims=True) for g in groups]
    o_w = [_dot(p_w[g], kvw[g])[:, HEAD_DIM:] / l_w[g] for g in groups]

    nq2 = NSA_GROUPS * NSA_Q_BLOCK
    p_t = p_slc.T
    blk = lax.broadcasted_iota(jnp.int32, (nb, nq2), 0)
    cand = (blk >= 1) & (blk <= blk_t - 2)
    c = jnp.where(cand, p_t, -jnp.inf)
    sel_t = jnp.zeros((nb, nq2), F32)
    for _ in range(SEL_TOP_N - N_FORCED):
        mx = jnp.max(c, axis=0, keepdims=True)
        first = jnp.min(jnp.where(c == mx, blk, nb), axis=0, keepdims=True)
        hit = blk == first
        sel_t = jnp.where(hit, 1.0, sel_t)
        c = jnp.where(hit, -jnp.inf, c)
    forced = (blk == 0) | (blk == blk_t) | (blk == blk_t - 1)
    all_causal = (blk <= blk_t) & (blk_t < SEL_TOP_N)
    sel_t = jnp.where(forced | all_causal, 1.0, sel_t)
    neg_tail_t = jnp.where(sel_t > 0, 0.0, NEG)
    neg_far_t = jnp.where(blk < blk_t - 2, neg_tail_t, NEG)
    neg_tail = neg_tail_t.T.astype(BF16)
    neg_far = neg_far_t.T.astype(BF16)
    q_far, q_tail = [], []
    for g in groups:
        r = slice(g * NSA_Q_BLOCK, (g + 1) * NSA_Q_BLOCK)
        q_far.append(jnp.concatenate([qg2[g], stack_heads(neg_far[r])], axis=1))
        q_tail.append(jnp.concatenate([qg2[g], stack_heads(neg_tail[r])], axis=1))

    def scores(q_aug, start, n_keys, bias):
        s = [_dot_nt(q_aug[g], slc_ref[0, g, pl.ds(start, n_keys), :]) for g in groups]
        if bias is not None:
            s = [s[g] + bias[g] for g in groups]
        return [s[g].astype(BF16) for g in groups]

    def update(carry, s, start, n_keys):
        kv = [slc_ref[0, g, pl.ds(start, n_keys), :] for g in groups]
        m_new = [jnp.maximum(carry[g][0], jnp.max(s[g], axis=1, keepdims=True).astype(F32)) for g in groups]
        p = [jnp.exp(s[g] - m_new[g].astype(BF16)) for g in groups]
        pv = [_dot(p[g], kv[g]) for g in groups]
        return tuple((m_new[g], jnp.exp(carry[g][0] - m_new[g]) * carry[g][1] + pv[g]) for g in groups)

    n_far = (jnp.maximum(blk_t - 2, 0) + FAR_TILE_BLOCKS - 1) // FAR_TILE_BLOCKS
    half = FAR_TILE // 2

    def stash(buf, s):
        for g in groups:
            buf[g] = s[g]

    stash(sa_ref, scores(q_far, KV_FRONT, half, None))

    def far_body(j, carry):
        a0 = pl.multiple_of(KV_FRONT + j * FAR_TILE, LANES)
        stash(sb_ref, scores(q_far, a0 + half, half, None))
        carry = update(carry, [sa_ref[g] for g in groups], a0, half)
        stash(sa_ref, scores(q_far, a0 + FAR_TILE, half, None))
        return update(carry, [sb_ref[g] for g in groups], a0 + half, half)

    init = (jnp.full((rows, 1), NEG, BF16).astype(F32), jnp.zeros((rows, 2 * HEAD_DIM + nb), F32))
    carry = lax.fori_loop(0, n_far, far_body, (init,) * NSA_GROUPS)
    tail0 = pl.multiple_of(KV_FRONT - SEL_NEAR + s0, NSA_Q_BLOCK)
    carry = update(carry, scores(q_tail, tail0, TAIL_KEYS, [btail_ref[var, g] for g in groups]),
                   tail0, TAIL_KEYS)

    for g in groups:
        acc = carry[g][1]
        o_s = acc[:, HEAD_DIM:2 * HEAD_DIM] / jnp.sum(acc[:, 2 * HEAD_DIM:], axis=1, keepdims=True)
        for h in range(HEADS_PER_GROUP):
            head = g * HEADS_PER_GROUP + h
            r = slice(h * NSA_Q_BLOCK, (h + 1) * NSA_Q_BLOCK)
            o = (gates[:, head:head + 1] * o_c[g][r]
                 + gates[:, NSA_HEADS + head:NSA_HEADS + head + 1] * o_s[r]
                 + gates[:, 2 * NSA_HEADS + head:2 * NSA_HEADS + head + 1] * o_w[g][r])
            o_ref[0, :, head * HEAD_DIM:(head + 1) * HEAD_DIM] = o.astype(BF16)


def _nsa(qn, gn, kvcmp, slc, win, mmat, btail, bwin):
    B, S, _ = qn.shape
    G = NSA_GROUPS
    nq = S // NSA_Q_BLOCK
    qb = NSA_Q_BLOCK
    return pl.pallas_call(
        _nsa_kernel,
        grid=(B, nq),
        in_specs=[pl.BlockSpec((1, qb, 512), lambda b, i: (b, i, 0)),
                  pl.BlockSpec((1, qb, LANES), lambda b, i: (b, i, 0)),
                  pl.BlockSpec((1, G) + kvcmp.shape[2:], lambda b, i: (b, 0, 0, 0)),
                  pl.BlockSpec((1, G) + slc.shape[2:], lambda b, i: (b, 0, 0, 0)),
                  pl.BlockSpec((1, G) + win.shape[2:], lambda b, i: (b, 0, 0, 0)),
                  pl.BlockSpec(mmat.shape, lambda b, i: (0, 0)),
                  pl.BlockSpec(btail.shape, lambda b, i: (0, 0, 0, 0)),
                  pl.BlockSpec(bwin.shape, lambda b, i: (0, 0, 0))],
        out_specs=pl.BlockSpec((1, qb, 512), lambda b, i: (b, i, 0)),
        out_shape=jax.ShapeDtypeStruct((B, S, 512), BF16),
        scratch_shapes=[pltpu.VMEM((G, HEADS_PER_GROUP * qb, FAR_TILE // 2), BF16)] * 2,
        compiler_params=pltpu.CompilerParams(dimension_semantics=("arbitrary", "arbitrary"),
                                             vmem_limit_bytes=VMEM_LIMIT),
    )(qn, gn, kvcmp, slc, win, mmat, btail, bwin)


def _swa_kernel(qs_ref, kv_ref, bias_ref, sink_ref, o_ref):
    qi = pl.program_id(1)
    s0 = qi * SWA_Q_BLOCK
    hg = SWA_HEADS // SWA_KV_HEADS
    q = qs_ref[0]
    var = jnp.minimum(qi, 1)
    groups = range(SWA_KV_HEADS)

    def group_rows(g):
        qg = jnp.concatenate([q[:, (g * hg + h) * HEAD_DIM:(g * hg + h + 1) * HEAD_DIM] for h in range(hg)],
                             axis=0)
        return jnp.concatenate([qg, jnp.zeros_like(qg)], axis=1)

    qg2 = [group_rows(g) for g in groups]
    kv0 = pl.multiple_of(KV_FRONT - SWA_WINDOW + s0, SWA_Q_BLOCK)
    kv = [kv_ref[0, g, pl.ds(kv0, SWA_KEYS), :] for g in groups]
    s = [_dot_nt(qg2[g], kv[g]) + bias_ref[var, g] for g in groups]
    sink = [sink_ref[g] for g in groups]
    m = [jnp.maximum(jnp.max(s[g], axis=1, keepdims=True), sink[g]) for g in groups]
    p = [jnp.exp(s[g] - m[g]) for g in groups]
    l = [jnp.sum(p[g], axis=1, keepdims=True) + jnp.exp(sink[g] - m[g]) for g in groups]
    o = [_dot(p[g].astype(BF16), kv[g])[:, HEAD_DIM:] / l[g] for g in groups]
    for g in groups:
        for h in range(hg):
            head = g * hg + h
            o_ref[0, :, head * HEAD_DIM:(head + 1) * HEAD_DIM] = (
                o[g][h * SWA_Q_BLOCK:(h + 1) * SWA_Q_BLOCK].astype(BF16))


def _swa(qs, kvp, bias, sink_rows):
    B, S, _ = qs.shape
    qb = SWA_Q_BLOCK
    return pl.pallas_call(
        _swa_kernel,
        grid=(B, S // qb),
        in_specs=[pl.BlockSpec((1, qb, 512), lambda b, i: (b, i, 0)),
                  pl.BlockSpec((1, SWA_KV_HEADS) + kvp.shape[2:], lambda b, i: (b, 0, 0, 0)),
                  pl.BlockSpec(bias.shape, lambda b, i: (0, 0, 0, 0)),
                  pl.BlockSpec(sink_rows.shape, lambda b, i: (0, 0, 0))],
        out_specs=pl.BlockSpec((1, qb, 512), lambda b, i: (b, i, 0)),
        out_shape=jax.ShapeDtypeStruct((B, S, 512), BF16),
        compiler_params=pltpu.CompilerParams(dimension_semantics=("arbitrary", "arbitrary"),
                                             vmem_limit_bytes=VMEM_LIMIT),
    )(qs, kvp, bias, sink_rows)


def _pack_bf16_pair(lo, hi):
    lo_b = pltpu.bitcast(lo.astype(BF16).astype(F32), jnp.uint32)
    hi_b = pltpu.bitcast(hi.astype(BF16).astype(F32), jnp.uint32)
    return (lo_b >> 16) | (hi_b & jnp.uint32(0xFFFF0000))


def _unpack_bf16_pair(w):
    lo = pltpu.bitcast(w << 16, F32).astype(BF16)
    hi = pltpu.bitcast(w & jnp.uint32(0xFFFF0000), F32).astype(BF16)
    return lo, hi


def _merge_kernel(on_ref, os_ref, gm_ref, x_ref, mod_ref, wb_ref, wo_ref, g_ref, wr_ref, br_ref, ltri_ref,
                  xo_ref, hp_ref, idx_ref, gate_ref, cnt_ref, run_ref):
    d = D_MODEL
    a = _dot(on_ref[...], wb_ref[0])
    b = _dot(os_ref[...], wb_ref[1])
    gm = gm_ref[...]
    merged = _sigmoid(gm[:, :d]) * a + _sigmoid(gm[:, d:]) * b
    y = _dot(merged.astype(BF16), wo_ref[...])
    xn = x_ref[...] + mod_ref[0, 2:3, :] * y
    xo_ref[...] = xn
    ms = jnp.mean(xn * xn, axis=-1, keepdims=True)
    h2 = xn * lax.rsqrt(ms + RMS_EPS) * g_ref[...]
    h2 = h2 * (1.0 + mod_ref[0, 4:5, :]) + mod_ref[0, 3:4, :]
    hp_ref[...] = _pack_bf16_pair(h2[:, :d // 2], h2[:, d // 2:])
    h_hi, h_mid, h_lo = _split3(h2)
    w_hi, w_mid, w_lo = wr_ref[0], wr_ref[1], wr_ref[2]
    logits = (_dot(h_hi, w_hi) + (_dot(h_hi, w_mid) + _dot(h_mid, w_hi))
              + (_dot(h_hi, w_lo) + _dot(h_mid, w_mid) + _dot(h_lo, w_hi))) + br_ref[...]
    tm = logits.shape[0]
    lane = lax.broadcasted_iota(jnp.int32, (tm, LANES), 1)
    c = jnp.where(lane < N_EXPERTS, logits, -jnp.inf)
    vals, idxs = [], []
    for _ in range(TOP_K):
        mx = jnp.max(c, axis=1, keepdims=True)
        first = jnp.min(jnp.where(c == mx, lane, LANES), axis=1, keepdims=True)
        vals.append(mx)
        idxs.append(first)
        c = jnp.where(lane == first, -jnp.inf, c)
    es = [jnp.exp(v - vals[0]) for v in vals]
    den = es[0] + es[1] + es[2] + es[3]

    @pl.when(pl.program_id(0) == 0)
    def _():
        run_ref[...] = jnp.zeros_like(run_ref)

    onehot = jnp.zeros((tm, LANES), F32)
    for k in range(TOP_K):
        onehot = jnp.where(lane == idxs[k], 1.0, onehot)
    before = _dot(ltri_ref[...], onehot.astype(BF16)) + run_ref[...]
    run_ref[...] = run_ref[...] + jnp.sum(onehot, axis=0, keepdims=True)
    cnt_ref[...] = run_ref[...].astype(jnp.int32)

    idx_out = jnp.zeros((tm, LANES), jnp.int32)
    gate_out = jnp.zeros((tm, LANES), F32)
    for k in range(TOP_K):
        rank = jnp.sum(jnp.where(lane == idxs[k], before, 0.0), axis=1, keepdims=True).astype(jnp.int32)
        idx_out = jnp.where(lane == k, idxs[k], idx_out)
        idx_out = jnp.where(lane == TOP_K + k, rank, idx_out)
        gate_out = jnp.where(lane == k, es[k] / den, gate_out)
    idx_ref[...] = idx_out
    gate_ref[...] = gate_out


def _merge(on, os_, gm, x, mod, wb, wo, g_ffn, wr3, br, S):
    T, D = x.shape
    tm = 256
    nb_per_seq = S // tm
    ltri = jnp.asarray(np.tril(np.ones((tm, tm), np.float32), -1), BF16)
    return pl.pallas_call(
        _merge_kernel,
        grid=(T // tm,),
        in_specs=[pl.BlockSpec((tm, 512), lambda i: (i, 0)),
                  pl.BlockSpec((tm, 512), lambda i: (i, 0)),
                  pl.BlockSpec((tm, 2 * D), lambda i: (i, 0)),
                  pl.BlockSpec((tm, D), lambda i: (i, 0)),
                  pl.BlockSpec((1, 6, D), lambda i: (i // nb_per_seq, 0, 0)),
                  pl.BlockSpec((2, 512, D), lambda i: (0, 0, 0)),
                  pl.BlockSpec((D, D), lambda i: (0, 0)),
                  pl.BlockSpec((1, D), lambda i: (0, 0)),
                  pl.BlockSpec((3, D, LANES), lambda i: (0, 0, 0)),
                  pl.BlockSpec((1, LANES), lambda i: (0, 0)),
                  pl.BlockSpec((tm, tm), lambda i: (0, 0))],
        out_specs=[pl.BlockSpec((tm, D), lambda i: (i, 0)),
                   pl.BlockSpec((tm, D // 2), lambda i: (i, 0)),
                   pl.BlockSpec((tm, LANES), lambda i: (i, 0)),
                   pl.BlockSpec((tm, LANES), lambda i: (i, 0)),
                   pl.BlockSpec((1, LANES), lambda i: (0, 0))],
        out_shape=[jax.ShapeDtypeStruct((T, D), F32),
                   jax.ShapeDtypeStruct((T, D // 2), jnp.uint32),
                   jax.ShapeDtypeStruct((T, LANES), jnp.int32),
                   jax.ShapeDtypeStruct((T, LANES), F32),
                   jax.ShapeDtypeStruct((1, LANES), jnp.int32)],
        scratch_shapes=[pltpu.VMEM((1, LANES), F32)],
        compiler_params=pltpu.CompilerParams(dimension_semantics=("arbitrary",),
                                             vmem_limit_bytes=VMEM_LIMIT),
    )(on, os_, gm, x, mod, wb, wo, g_ffn, wr3, br, ltri)


def _dispatch_kernel(dest_ref, hp_ref, xs_in_hbm, xs_hbm, sem):
    del xs_in_hbm
    tm = hp_ref.shape[0]

    def body(r, _):
        for k in range(TOP_K):
            pltpu.make_async_copy(hp_ref.at[pl.ds(r, 1), :],
                                  xs_hbm.at[pl.ds(dest_ref[0, 0, r * TOP_K + k], 1), :], sem).start()
        return 0
    lax.fori_loop(0, tm, body, 0, unroll=DMA_LOOP_UNROLL)
    for _ in range(TOP_K):
        pltpu.make_async_copy(hp_ref, xs_hbm.at[pl.ds(0, tm), :], sem).wait()


def _dispatch(dest, hp, n_slots):
    T, W = hp.shape
    tm = 256
    xs0 = jnp.zeros((n_slots, W), hp.dtype)
    return pl.pallas_call(
        _dispatch_kernel,
        grid=(T // tm,),
        in_specs=[pl.BlockSpec((1, 1, tm * TOP_K), lambda i: (i, 0, 0), memory_space=pltpu.SMEM),
                  pl.BlockSpec((tm, W), lambda i: (i, 0)),
                  pl.BlockSpec(memory_space=pl.ANY)],
        out_specs=pl.BlockSpec(memory_space=pl.ANY),
        out_shape=jax.ShapeDtypeStruct((n_slots, W), hp.dtype),
        scratch_shapes=[pltpu.SemaphoreType.DMA(())],
        input_output_aliases={2: 0},
        compiler_params=pltpu.CompilerParams(dimension_semantics=("arbitrary",),
                                             vmem_limit_bytes=VMEM_LIMIT),
    )(dest.reshape(T // tm, 1, tm * TOP_K), hp, xs0)


def _moe_kernel(bexp_ref, used_ref, xs_ref, wu_ref, wd_ref, bu_ref, bd_ref, ys_ref):
    i = pl.program_id(0)

    @pl.when(used_ref[i] > 0)
    def _():
        x_lo, x_hi = _unpack_bf16_pair(xs_ref[...])
        half = D_MODEL // 2
        hid = []
        for t in range(2 * D_EXPERT // UP_TILE):
            cols = slice(t * UP_TILE, (t + 1) * UP_TILE)
            gu = _dot(x_lo, wu_ref[0, :half, cols]) + _dot(x_hi, wu_ref[0, half:, cols]) + bu_ref[0, :, cols]
            glu = jnp.minimum(gu[:, :UP_TILE // 2], SWIGLU_LIMIT)
            lin = jnp.clip(gu[:, UP_TILE // 2:], -SWIGLU_LIMIT, SWIGLU_LIMIT)
            hid.append((glu * _sigmoid(SWIGLU_ALPHA * glu) * (lin + 1.0)).astype(BF16))
        ys_ref[...] = _dot(jnp.concatenate(hid, axis=1), wd_ref[0]) + bd_ref[0]

    @pl.when(used_ref[i] == 0)
    def _():
        ys_ref[...] = jnp.zeros_like(ys_ref)


def _wup_prep_kernel(w_ref, p_ref, o_ref):
    for t in range(w_ref.shape[2] // UP_TILE):
        cols = slice(t * UP_TILE, (t + 1) * UP_TILE)
        o_ref[0, :, cols] = _dot(w_ref[0, :, cols].astype(BF16), p_ref[...]).astype(BF16)


def _wup_prep(w_up):
    N, D, W = w_up.shape
    j = np.arange(UP_TILE)
    perm = np.zeros((UP_TILE, UP_TILE), np.float32)
    perm[j, (j % 2) * (UP_TILE // 2) + j // 2] = 1.0
    return pl.pallas_call(
        _wup_prep_kernel,
        grid=(N,),
        in_specs=[pl.BlockSpec((1, D, W), lambda n: (n, 0, 0)),
                  pl.BlockSpec((UP_TILE, UP_TILE), lambda n: (0, 0))],
        out_specs=pl.BlockSpec((1, D, W), lambda n: (n, 0, 0)),
        out_shape=jax.ShapeDtypeStruct((N, D, W), BF16),
        compiler_params=pltpu.CompilerParams(dimension_semantics=("arbitrary",),
                                             vmem_limit_bytes=VMEM_LIMIT),
    )(w_up, jnp.asarray(perm, BF16))


def _moe(bexp, used, xs, wu, wd, bu, bd):
    nblk = bexp.shape[0]
    D = D_MODEL
    tm = MOE_BLOCK
    grid_spec = pltpu.PrefetchScalarGridSpec(
        num_scalar_prefetch=2,
        grid=(nblk,),
        in_specs=[pl.BlockSpec((tm, D // 2), lambda i, be, us: (i, 0)),
                  pl.BlockSpec((1, D, 2 * D_EXPERT), lambda i, be, us: (be[i], 0, 0)),
                  pl.BlockSpec((1, D_EXPERT, D), lambda i, be, us: (be[i], 0, 0)),
                  pl.BlockSpec((1, 1, 2 * D_EXPERT), lambda i, be, us: (be[i], 0, 0)),
                  pl.BlockSpec((1, 1, D), lambda i, be, us: (be[i], 0, 0))],
        out_specs=pl.BlockSpec((tm, D), lambda i, be, us: (i, 0)),
    )
    return pl.pallas_call(
        _moe_kernel,
        grid_spec=grid_spec,
        out_shape=jax.ShapeDtypeStruct((nblk * tm, D), F32),
        compiler_params=pltpu.CompilerParams(dimension_semantics=("arbitrary",),
                                             vmem_limit_bytes=VMEM_LIMIT),
    )(bexp, used, xs, wu, wd, bu, bd)


def _combine_kernel(dest_ref, destn_ref, x_ref, gate_ref, mod_ref, g_ref, ys_hbm, o_ref, ybuf, sem, *, final):
    i = pl.program_id(0)
    nt = pl.num_programs(0)
    slot = i % 2
    tm = x_ref.shape[0]

    def issue(ids_ref, sl):
        def body(r, _):
            for k in range(TOP_K):
                pltpu.make_async_copy(ys_hbm.at[pl.ds(ids_ref[0, 0, r * TOP_K + k], 1), :],
                                      ybuf.at[sl, k, pl.ds(r, 1), :], sem.at[sl]).start()
            return 0
        lax.fori_loop(0, tm, body, 0, unroll=DMA_LOOP_UNROLL)

    @pl.when(i == 0)
    def _():
        issue(dest_ref, 0)

    @pl.when(i + 1 < nt)
    def _():
        issue(destn_ref, 1 - slot)

    for k in range(TOP_K):
        pltpu.make_async_copy(ys_hbm.at[pl.ds(0, tm), :], ybuf.at[slot, k], sem.at[slot]).wait()

    gate = gate_ref[...]
    y = gate[:, 0:1] * ybuf[slot, 0]
    for k in range(1, TOP_K):
        y = y + gate[:, k:k + 1] * ybuf[slot, k]
    xn = x_ref[...] + mod_ref[0, 5:6, :] * y
    if final:
        ms = jnp.mean(xn * xn, axis=-1, keepdims=True)
        xn = xn * lax.rsqrt(ms + RMS_EPS) * g_ref[...]
    o_ref[...] = xn


def _combine(dest, x, ys, gate, mod, g_final, S, final):
    T, D = x.shape
    tm = 256
    nb_per_seq = S // tm
    nt = T // tm
    dest3 = dest.reshape(nt, 1, tm * TOP_K)
    smem_blk = lambda f: pl.BlockSpec((1, 1, tm * TOP_K), f, memory_space=pltpu.SMEM)
    return pl.pallas_call(
        functools.partial(_combine_kernel, final=final),
        grid=(nt,),
        in_specs=[smem_blk(lambda i: (i, 0, 0)),
                  smem_blk(lambda i: (jnp.minimum(i + 1, nt - 1), 0, 0)),
                  pl.BlockSpec((tm, D), lambda i: (i, 0)),
                  pl.BlockSpec((tm, LANES), lambda i: (i, 0)),
                  pl.BlockSpec((1, 6, D), lambda i: (i // nb_per_seq, 0, 0)),
                  pl.BlockSpec((1, D), lambda i: (0, 0)),
                  pl.BlockSpec(memory_space=pl.ANY)],
        out_specs=pl.BlockSpec((tm, D), lambda i: (i, 0)),
        out_shape=jax.ShapeDtypeStruct((T, D), F32),
        scratch_shapes=[pltpu.VMEM((2, TOP_K, tm, D), F32),
                        pltpu.SemaphoreType.DMA((2,))],
        compiler_params=pltpu.CompilerParams(dimension_semantics=("arbitrary",),
                                             vmem_limit_bytes=VMEM_LIMIT),
    )(dest3, dest3, x, gate, mod, g_final, ys)


def _t5_bucket(dist):
    n = jnp.maximum(dist, 0)
    max_exact = REL_BUCKETS // 2
    log_ratio = jnp.log(jnp.maximum(n, 1).astype(F32) / max_exact) / math.log(REL_MAX_DIST / max_exact)
    large = max_exact + (log_ratio * (REL_BUCKETS - max_exact)).astype(jnp.int32)
    return jnp.where(n < max_exact, n, jnp.minimum(large, REL_BUCKETS - 1))


def _toeplitz_bias(tab, q_block, n_keys, back, window, variants, shift_far):
    r = jnp.arange(q_block)[:, None]
    c = jnp.arange(n_keys)[None, :]
    dist = r + back - c
    ok = dist >= 0
    if window is not None:
        ok = ok & (dist < window)
    bias = tab[_t5_bucket(dist)]
    if shift_far:
        bias = bias - tab[REL_BUCKETS - 1][None, None]
    bias = jnp.transpose(bias, (2, 3, 0, 1))
    outs = []
    for v in range(variants):
        okv = ok if v == variants - 1 else ok & (c >= back - v * q_block)
        outs.append(jnp.where(okv[None, None], bias, NEG))
    out = jnp.stack(outs)
    V, G, Hg = out.shape[:3]
    return out.reshape(V, G, Hg * q_block, n_keys).astype(F32)


def _routing_tables(route, counts, T):
    E = N_EXPERTS
    n_blocks = -(-(T * TOP_K) // MOE_BLOCK) + E
    padded = (counts + MOE_BLOCK - 1) // MOE_BLOCK * MOE_BLOCK
    pad_end = jnp.cumsum(padded)
    grp_start = pad_end - padded
    e_idx = route[:, :TOP_K]
    onehot = e_idx[..., None] == jnp.arange(E, dtype=jnp.int32)
    dest = jnp.sum(jnp.where(onehot, grp_start, 0), axis=-1) + route[:, TOP_K:2 * TOP_K]
    blk_start = jnp.arange(n_blocks, dtype=jnp.int32) * MOE_BLOCK
    bexp = jnp.minimum(jnp.sum(pad_end[None, :] <= blk_start[:, None], axis=1), E - 1).astype(jnp.int32)
    used = (blk_start < pad_end[-1]).astype(jnp.int32)
    return dest.astype(jnp.int32), bexp, used


def kernel(x, c, w_ada, b_ada, g_mix, g_ffn, g_final, w_in, cmp_pos, cmp_w1, cmp_w2, sinks, rel_tab,
           w_branch, w_out, w_router, b_router, w_up, b_up, w_down, b_down):
    B, S, D = x.shape
    L = w_ada.shape[0]
    T = B * S
    G = NSA_GROUPS
    nb = S // SEL_BLOCK
    ncp = S // CMP_STRIDE

    w_in_p = _reorder_in_proj(w_in)
    pos_r = cmp_pos.reshape(L, 2, 2, 1, CMP_STRIDE * HEAD_DIM)
    w1_r = cmp_w1.reshape(L, 2, 2, CMP_STRIDE * HEAD_DIM, CMP_HIDDEN).astype(BF16)
    w2_r = cmp_w2.astype(BF16)
    wb = w_branch.astype(BF16)
    wo = w_out.astype(BF16)
    wr_pad = jnp.pad(w_router, ((0, 0), (0, 0), (0, LANES - N_EXPERTS)))
    wr_hi = wr_pad.astype(BF16)
    wr_r1 = wr_pad - wr_hi.astype(F32)
    wr_mid = wr_r1.astype(BF16)
    wr_lo = (wr_r1 - wr_mid.astype(F32)).astype(BF16)
    wr3 = jnp.stack([wr_hi, wr_mid, wr_lo], axis=1)
    br = jnp.pad(b_router, ((0, 0), (0, LANES - N_EXPERTS))).reshape(L, 1, LANES)
    wu = _wup_prep(w_up.reshape(L * N_EXPERTS, D, 2 * D_EXPERT)).reshape(L, N_EXPERTS, D, 2 * D_EXPERT)
    n_up_tiles = 2 * D_EXPERT // UP_TILE
    bu = b_up.reshape(L, N_EXPERTS, n_up_tiles, UP_TILE // 2, 2).transpose(0, 1, 2, 4, 3).reshape(
        L, N_EXPERTS, 1, 2 * D_EXPERT)
    wd = w_down.astype(BF16)
    bd = b_down.reshape(L, N_EXPERTS, 1, D)

    nsa_tab = rel_tab[:, :NSA_HEADS].reshape(REL_BUCKETS, G, HEADS_PER_GROUP)
    swa_tab = rel_tab[:, NSA_HEADS:].reshape(REL_BUCKETS, SWA_KV_HEADS, SWA_HEADS // SWA_KV_HEADS)
    btail = _toeplitz_bias(nsa_tab, NSA_Q_BLOCK, TAIL_KEYS, SEL_NEAR, None, 3, True)
    bwin = _toeplitz_bias(nsa_tab, NSA_Q_BLOCK, WIN_KEYS, NSA_WINDOW, NSA_WINDOW, 1, False)[0]
    bswa = _toeplitz_bias(swa_tab, SWA_Q_BLOCK, SWA_KEYS, SWA_WINDOW, SWA_WINDOW, 2, False)
    ci = jnp.arange(ncp)[:, None]
    bj = jnp.arange(nb)[None, :]
    delta = (SEL_BLOCK // CMP_STRIDE) * bj - ci
    mmat = jnp.where((delta == 0) | (delta == 4), 1.0, jnp.where((delta >= 1) & (delta <= 3), 2.0, 0.0))
    mmat = jnp.where(ci < ncp - 1, mmat, 0.0).astype(BF16)
    slc_rows = KV_FRONT + S + FAR_TILE // 2
    kpos = jnp.arange(slc_rows)[:, None] - KV_FRONT
    onehot_blk = ((kpos >= 0) & (kpos < S) & (kpos // SEL_BLOCK == jnp.arange(nb)[None, :])).astype(BF16)
    slc_t = jnp.broadcast_to(
        jnp.concatenate([jnp.zeros((slc_rows, 2 * HEAD_DIM), BF16), onehot_blk], axis=1),
        (B, G, slc_rows, 2 * HEAD_DIM + nb))
    win_t = jnp.zeros((B, G, KV_FRONT + S, 2 * HEAD_DIM), BF16)

    c_pad = jnp.pad(c, ((0, 8 - B), (0, 0)))
    mod_all = _ada_mod(c_pad, w_ada, b_ada)[:, :B].reshape(L, B, 6, D)

    xf = x
    for l in range(L):
        mod = mod_all[l]
        qn, kvc, slc, win, gn, qs, swa_kv, gm = _inproj(xf, mod, g_mix[l].reshape(1, D), w_in_p[l],
                                                        slc_t, win_t, win_t)
        xc = kvc.reshape(B, 2 * G, ncp, CMP_STRIDE * HEAD_DIM)
        kvcmp = _compress(xc, pos_r[l], w1_r[l], w2_r[l])
        o_n = _nsa(qn, gn, kvcmp, slc, win, mmat, btail, bwin)
        sink_rows = jnp.repeat(sinks[l].reshape(SWA_KV_HEADS, SWA_HEADS // SWA_KV_HEADS), SWA_Q_BLOCK,
                               axis=1)[..., None]
        o_s = _swa(qs, swa_kv, bswa, sink_rows)
        x2, hp, route, gate, counts = _merge(o_n.reshape(T, 512), o_s.reshape(T, 512), gm.reshape(T, 2 * D),
                                             xf.reshape(T, D), mod, wb[l], wo[l], g_ffn[l].reshape(1, D),
                                             wr3[l], br[l], S)
        dest, bexp, used = _routing_tables(route, counts[0, :N_EXPERTS], T)
        xs = _dispatch(dest, hp, bexp.shape[0] * MOE_BLOCK)
        ys = _moe(bexp, used, xs, wu[l], wd[l], bu[l], bd[l])
        xf = _combine(dest, x2, ys, gate, mod, g_final.reshape(1, D), S,
                      final=(l == L - 1)).reshape(B, S, D)
    return xf
```

```python
import functools
import math

import numpy as np
import jax
import jax.numpy as jnp
from jax import lax
from jax.experimental import pallas as pl
from jax.experimental.pallas import tpu as pltpu

D_MODEL = 1024
HEAD_DIM = 64
NSA_HEADS = 8
NSA_GROUPS = 2
HEADS_PER_GROUP = NSA_HEADS // NSA_GROUPS
CMP_LEN = 32
CMP_STRIDE = 16
CMP_HIDDEN = 256
SEL_BLOCK = 64
SEL_TOP_N = 16
NSA_WINDOW = 512
NSA_Q_BLOCK = 64
SWA_HEADS = 8
SWA_KV_HEADS = 2
SWA_WINDOW = 128
SWA_Q_BLOCK = 128
REL_BUCKETS = 32
REL_MAX_DIST = 128
N_EXPERTS = 32
TOP_K = 4
D_EXPERT = 1024
SWIGLU_ALPHA = 1.702
SWIGLU_LIMIT = 7.0
MOE_BLOCK = 256
RMS_EPS = 1e-5

LANES = 128
NEG = -1e30
FAR_TILE = 1024
FAR_TILE_BLOCKS = FAR_TILE // SEL_BLOCK
SEL_NEAR = 2 * SEL_BLOCK
TAIL_KEYS = SEL_NEAR + NSA_Q_BLOCK
WIN_KEYS = NSA_WINDOW + NSA_Q_BLOCK
SWA_KEYS = SWA_WINDOW + SWA_Q_BLOCK
KV_FRONT = 512
N_FORCED = 3
UP_TILE = 256
DMA_LOOP_UNROLL = 8
VMEM_LIMIT = 56 * 1024 * 1024

F32 = jnp.float32
BF16 = jnp.bfloat16

_OFF_QN = 0
_OFF_KVN = 512
_OFF_GN = 1280
_OFF_QS = 1304
_OFF_KVS = 1816
_OFF_GM = 2072
_D_IN = 4120
_P_QN, _P_CMP, _P_SLC, _P_WIN, _P_GN, _P_QS, _P_SWA, _P_GM, _P_END = (
    0, 512, 768, 1024, 1280, 1408, 1920, 2176, 4224)


def _sigmoid(v):
    return 1.0 / (1.0 + jnp.exp(-v))


def _dot(a, b):
    return jnp.dot(a, b, preferred_element_type=F32)


def _dot_nt(a, b):
    return lax.dot_general(a, b, (((1,), (1,)), ((), ())), preferred_element_type=F32)


def _split3(v):
    hi = v.astype(BF16)
    r1 = v - hi.astype(F32)
    mid = r1.astype(BF16)
    lo = (r1 - mid.astype(F32)).astype(BF16)
    return hi, mid, lo


def _reorder_in_proj(w_in):
    def cols(a, n):
        return w_in[..., a:a + n]

    def kv_pairs(base):
        return [cols(base + off, 64) for off in (0, 128, 64, 192)]

    parts = [cols(_OFF_QN, 512)]
    parts += kv_pairs(_OFF_KVN)
    parts += kv_pairs(_OFF_KVN + 256)
    parts += kv_pairs(_OFF_KVN + 512)
    parts += [cols(_OFF_GN, 24), jnp.zeros(w_in.shape[:-1] + (LANES - 24,), w_in.dtype)]
    parts += [cols(_OFF_QS, 512)]
    parts += kv_pairs(_OFF_KVS)
    parts += [cols(_OFF_GM, 2048)]
    out = jnp.concatenate(parts, axis=-1).astype(BF16)
    assert out.shape[-1] == _P_END
    return out


def _ada_kernel(c_ref, w_ref, b_ref, o_ref):
    c = c_ref[...]
    cond = c * _sigmoid(c)
    o_ref[0] = _dot(cond.astype(BF16), w_ref[0].astype(BF16)) + b_ref[0]


def _ada_mod(c_pad, w_ada, b_ada):
    L, D, N = w_ada.shape
    tn = 1536
    return pl.pallas_call(
        _ada_kernel,
        grid=(L, N // tn),
        in_specs=[pl.BlockSpec((8, D), lambda l, j: (0, 0)),
                  pl.BlockSpec((1, D, tn), lambda l, j: (l, 0, j)),
                  pl.BlockSpec((1, 1, tn), lambda l, j: (l, 0, j))],
        out_specs=pl.BlockSpec((1, 8, tn), lambda l, j: (l, 0, j)),
        out_shape=jax.ShapeDtypeStruct((L, 8, N), F32),
        compiler_params=pltpu.CompilerParams(dimension_semantics=("arbitrary", "arbitrary"),
                                             vmem_limit_bytes=VMEM_LIMIT),
    )(c_pad, w_ada, b_ada.reshape(L, 1, N))


def _inproj_kernel(x_ref, mod_ref, g_ref, w_ref, slc_in, win_in, swa_in,
                   qn_ref, kvc_ref, slc_ref, win_ref, gn_ref, qs_ref, swa_ref, gm_ref):
    del slc_in, win_in, swa_in
    x = x_ref[0]
    ms = jnp.mean(x * x, axis=-1, keepdims=True)
    y = x * lax.rsqrt(ms + RMS_EPS) * g_ref[...]
    h = (y * (1.0 + mod_ref[0, 1:2, :]) + mod_ref[0, 0:1, :]).astype(BF16)

    def proj(a, b):
        return _dot(h, w_ref[:, a:b])

    def per_group(ref, vals, width):
        for c in range(vals.shape[1] // width):
            ref[0, c] = vals[:, c * width:(c + 1) * width].astype(ref.dtype)

    scale = HEAD_DIM ** -0.5
    qn_ref[0] = (proj(_P_QN, _P_CMP) * scale).astype(BF16)
    per_group(kvc_ref, proj(_P_CMP, _P_SLC), HEAD_DIM)
    per_group(slc_ref, proj(_P_SLC, _P_WIN), 2 * HEAD_DIM)
    per_group(win_ref, proj(_P_WIN, _P_GN), 2 * HEAD_DIM)
    gn_ref[0] = proj(_P_GN, _P_QS)
    qs_ref[0] = (proj(_P_QS, _P_SWA) * scale).astype(BF16)
    per_group(swa_ref, proj(_P_SWA, _P_GM), 2 * HEAD_DIM)
    gm_ref[0] = proj(_P_GM, _P_END)


def _inproj(x, mod, g, w, slc_t, win_t, swa_t):
    B, S, D = x.shape
    G = NSA_GROUPS
    tm = KV_FRONT
    front = KV_FRONT // tm
    plain = lambda n: pl.BlockSpec((1, tm, n), lambda b, i: (b, i, 0))
    padded = pl.BlockSpec((1, G, tm, 2 * HEAD_DIM), lambda b, i: (b, 0, i + front, 0))
    hbm = pl.BlockSpec(memory_space=pl.ANY)
    sds = jax.ShapeDtypeStruct
    return pl.pallas_call(
        _inproj_kernel,
        grid=(B, S // tm),
        in_specs=[pl.BlockSpec((1, tm, D), lambda b, i: (b, i, 0)),
                  pl.BlockSpec((1, 6, D), lambda b, i: (b, 0, 0)),
                  pl.BlockSpec((1, D), lambda b, i: (0, 0)),
                  pl.BlockSpec((D, _P_END), lambda b, i: (0, 0)),
                  hbm, hbm, hbm],
        out_specs=[plain(512),
                   pl.BlockSpec((1, 2 * G, tm, HEAD_DIM), lambda b, i: (b, 0, i, 0)),
                   padded, padded, plain(LANES), plain(512), padded, plain(2 * D)],
        out_shape=[sds((B, S, 512), BF16), sds((B, 2 * G, S, HEAD_DIM), F32),
                   sds(slc_t.shape, BF16), sds(win_t.shape, BF16), sds((B, S, LANES), F32),
                   sds((B, S, 512), BF16), sds(swa_t.shape, BF16), sds((B, S, 2 * D), F32)],
        input_output_aliases={4: 2, 5: 3, 6: 6},
        compiler_params=pltpu.CompilerParams(dimension_semantics=("arbitrary", "arbitrary"),
                                             vmem_limit_bytes=VMEM_LIMIT),
    )(x, mod, g, w, slc_t, win_t, swa_t)


def _gelu_tanh(v):
    return 0.5 * v * (1.0 + jnp.tanh(math.sqrt(2.0 / math.pi) * (v + 0.044715 * (v * v * v))))


def _compress_kernel(x_ref, pos_ref, w1_ref, w2_ref, o_ref):
    nch = x_ref.shape[2]
    row = lax.broadcasted_iota(jnp.int32, (nch, 1), 0)
    outs = []
    for kv in range(2):
        xc = x_ref[0, kv]
        a = _dot((xc + pos_ref[kv, 0]).astype(BF16), w1_ref[kv, 0])
        b = _dot((xc + pos_ref[kv, 1]).astype(BF16), w1_ref[kv, 1])
        b_next = pltpu.roll(b, nch - 1, axis=0)
        pre = a + jnp.where(row < nch - 1, b_next, 0.0)
        outs.append(_dot(_gelu_tanh(pre).astype(BF16), w2_ref[kv]))
    o_ref[0, 0] = jnp.concatenate(outs, axis=-1).astype(BF16)


def _compress(xc, pos, w1, w2):
    B, _, nch, cw = xc.shape
    return pl.pallas_call(
        _compress_kernel,
        grid=(B, NSA_GROUPS),
        in_specs=[pl.BlockSpec((1, 2, nch, cw), lambda b, g: (b, g, 0, 0)),
                  pl.BlockSpec((2, 2, 1, cw), lambda b, g: (0, 0, 0, 0)),
                  pl.BlockSpec((2, 2, cw, CMP_HIDDEN), lambda b, g: (0, 0, 0, 0)),
                  pl.BlockSpec((2, CMP_HIDDEN, HEAD_DIM), lambda b, g: (0, 0, 0))],
        out_specs=pl.BlockSpec((1, 1, nch, 2 * HEAD_DIM), lambda b, g: (b, g, 0, 0)),
        out_shape=jax.ShapeDtypeStruct((B, NSA_GROUPS, nch, 2 * HEAD_DIM), BF16),
        compiler_params=pltpu.CompilerParams(dimension_semantics=("arbitrary", "arbitrary"),
                                             vmem_limit_bytes=VMEM_LIMIT),
    )(xc, pos, w1, w2)


def _nsa_kernel(qn_ref, gn_ref, cmp_ref, slc_ref, win_ref, mmat_ref, btail_ref, bwin_ref, o_ref):
    qi = pl.program_id(1)
    s0 = qi * NSA_Q_BLOCK
    blk_t = qi
    ncp = cmp_ref.shape[2]
    nb = mmat_ref.shape[1]
    rows = HEADS_PER_GROUP * NSA_Q_BLOCK
    q = qn_ref[0]
    gates = _sigmoid(gn_ref[0])
    var = jnp.minimum(qi, 2)

    def stack_heads(v):
        return jnp.concatenate([v] * HEADS_PER_GROUP, axis=0)

    groups = range(NSA_GROUPS)
    col_c = lax.broadcasted_iota(jnp.int32, (NSA_Q_BLOCK, ncp), 1)
    t_c = s0 + lax.broadcasted_iota(jnp.int32, (NSA_Q_BLOCK, ncp), 0)
    bias_c = stack_heads(jnp.where(col_c * CMP_STRIDE + (CMP_LEN - 1) <= t_c, 0.0, NEG))
    col_w = lax.broadcasted_iota(jnp.int32, (1, WIN_KEYS), 1)
    front_w = jnp.where(col_w >= NSA_WINDOW - s0, 0.0, NEG)
    def group_rows(g):
        qg = jnp.concatenate(
            [q[:, (g * HEADS_PER_GROUP + h) * HEAD_DIM:(g * HEADS_PER_GROUP + h + 1) * HEAD_DIM]
             for h in range(HEADS_PER_GROUP)], axis=0)
        return jnp.concatenate([qg, jnp.zeros_like(qg)], axis=1)

    qg2 = [group_rows(g) for g in groups]

    kvc = [cmp_ref[0, g] for g in groups]
    s_c = [_dot_nt(qg2[g], kvc[g]) + bias_c for g in groups]
    m_c = [jnp.max(s_c[g], axis=1, keepdims=True) for g in groups]
    e_c = [jnp.exp(s_c[g] - m_c[g]) for g in groups]
    d_c = [jnp.sum(e_c[g], axis=1, keepdims=True) for g in groups]
    p_c = [e_c[g] * jnp.where(m_c[g] > 0.5 * NEG, 1.0 / d_c[g], 0.0) for g in groups]
    o_c = [_dot(p_c[g].astype(BF16), kvc[g])[:, HEAD_DIM:] for g in groups]
    imp = [(p_c[g][0:NSA_Q_BLOCK] + p_c[g][NSA_Q_BLOCK:2 * NSA_Q_BLOCK]
            + p_c[g][2 * NSA_Q_BLOCK:3 * NSA_Q_BLOCK] + p_c[g][3 * NSA_Q_BLOCK:]) for g in groups]
    i_hi, i_mid, i_lo = _split3(jnp.concatenate(imp, axis=0))
    mm = mmat_ref[...]
    p_slc = _dot(i_hi, mm) + _dot(i_mid, mm) + _dot(i_lo, mm)

    win0 = pl.multiple_of(KV_FRONT - NSA_WINDOW + s0, NSA_Q_BLOCK)
    kvw = [win_ref[0, g, pl.ds(win0, WIN_KEYS), :] for g in groups]
    s_w = [(_dot_nt(qg2[g], kvw[g]) + bwin_ref[g] + front_w).astype(BF16) for g in groups]
    m_w = [jnp.max(s_w[g], axis=1, keepdims=True) for g in groups]
    p_w = [jnp.exp(s_w[g] - m_w[g]) for g in groups]
    l_w = [jnp.sum(p_w[g].astype(F32), axis=1, keepdims=True) for g in groups]
    o_w = [_dot(p_w[g], kvw[g])[:, HEAD_DIM:] / l_w[g] for g in groups]

    nq2 = NSA_GROUPS * NSA_Q_BLOCK
    p_t = p_slc.T
    blk = lax.broadcasted_iota(jnp.int32, (nb, nq2), 0)
    cand = (blk >= 1) & (blk <= blk_t - 2)
    c = jnp.where(cand, p_t, -jnp.inf)
    sel_t = jnp.zeros((nb, nq2), F32)
    for _ in range(SEL_TOP_N - N_FORCED):
        mx = jnp.max(c, axis=0, keepdims=True)
        first = jnp.min(jnp.where(c == mx, blk, nb), axis=0, keepdims=True)
        hit = blk == first
        sel_t = jnp.where(hit, 1.0, sel_t)
        c = jnp.where(hit, -jnp.inf, c)
    forced = (blk == 0) | (blk == blk_t) | (blk == blk_t - 1)
    all_causal = (blk <= blk_t) & (blk_t < SEL_TOP_N)
    sel_t = jnp.where(forced | all_causal, 1.0, sel_t)
    neg_tail_t = jnp.where(sel_t > 0, 0.0, NEG)
    neg_far_t = jnp.where(blk < blk_t - 2, neg_tail_t, NEG)
    neg_tail = neg_tail_t.T.astype(BF16)
    neg_far = neg_far_t.T.astype(BF16)
    q_far, q_tail = [], []
    for g in groups:
        r = slice(g * NSA_Q_BLOCK, (g + 1) * NSA_Q_BLOCK)
        q_far.append(jnp.concatenate([qg2[g], stack_heads(neg_far[r])], axis=1))
        q_tail.append(jnp.concatenate([qg2[g], stack_heads(neg_tail[r])], axis=1))

    def scores(q_aug, start, n_keys, bias):
        s = [_dot_nt(q_aug[g], slc_ref[0, g, pl.ds(start, n_keys), :]) for g in groups]
        if bias is not None:
            s = [s[g] + bias[g] for g in groups]
        return [s[g].astype(BF16) for g in groups]

    def update(carry, s, start, n_keys):
        kv = [slc_ref[0, g, pl.ds(start, n_keys), :] for g in groups]
        m_new = [jnp.maximum(carry[g][0], jnp.max(s[g], axis=1, keepdims=True).astype(F32)) for g in groups]
        p = [jnp.exp(s[g] - m_new[g].astype(BF16)) for g in groups]
        pv = [_dot(p[g], kv[g]) for g in groups]
        return tuple((m_new[g], jnp.exp(carry[g][0] - m_new[g]) * carry[g][1] + pv[g]) for g in groups)

    n_far = (jnp.maximum(blk_t - 2, 0) + FAR_TILE_BLOCKS - 1) // FAR_TILE_BLOCKS

    n_pairs = n_far // 2

    def far_pair(j, carry):
        a0 = pl.multiple_of(KV_FRONT + j * (2 * FAR_TILE), LANES)
        return update(carry, scores(q_far, a0, 2 * FAR_TILE, None), a0, 2 * FAR_TILE)

    def far_body(j, carry):
        a0 = pl.multiple_of(KV_FRONT + j * FAR_TILE, LANES)
        return update(carry, scores(q_far, a0, FAR_TILE, None), a0, FAR_TILE)

    init = (jnp.full((rows, 1), NEG, BF16).astype(F32), jnp.zeros((rows, 2 * HEAD_DIM + nb), F32))
    carry = lax.fori_loop(0, n_pairs, far_pair, (init,) * NSA_GROUPS)
    carry = lax.fori_loop(2 * n_pairs, n_far, far_body, carry)
    tail0 = pl.multiple_of(KV_FRONT - SEL_NEAR + s0, NSA_Q_BLOCK)
    carry = update(carry, scores(q_tail, tail0, TAIL_KEYS, [btail_ref[var, g] for g in groups]),
                   tail0, TAIL_KEYS)

    for g in groups:
        acc = carry[g][1]
        o_s = acc[:, HEAD_DIM:2 * HEAD_DIM] / jnp.sum(acc[:, 2 * HEAD_DIM:], axis=1, keepdims=True)
        for h in range(HEADS_PER_GROUP):
            head = g * HEADS_PER_GROUP + h
            r = slice(h * NSA_Q_BLOCK, (h + 1) * NSA_Q_BLOCK)
            o = (gates[:, head:head + 1] * o_c[g][r]
                 + gates[:, NSA_HEADS + head:NSA_HEADS + head + 1] * o_s[r]
                 + gates[:, 2 * NSA_HEADS + head:2 * NSA_HEADS + head + 1] * o_w[g][r])
            o_ref[0, :, head * HEAD_DIM:(head + 1) * HEAD_DIM] = o.astype(BF16)


def _nsa(qn, gn, kvcmp, slc, win, mmat, btail, bwin):
    B, S, _ = qn.shape
    G = NSA_GROUPS
    nq = S // NSA_Q_BLOCK
    qb = NSA_Q_BLOCK
    return pl.pallas_call(
        _nsa_kernel,
        grid=(B, nq),
        in_specs=[pl.BlockSpec((1, qb, 512), lambda b, i: (b, i, 0)),
                  pl.BlockSpec((1, qb, LANES), lambda b, i: (b, i, 0)),
                  pl.BlockSpec((1, G) + kvcmp.shape[2:], lambda b, i: (b, 0, 0, 0)),
                  pl.BlockSpec((1, G) + slc.shape[2:], lambda b, i: (b, 0, 0, 0)),
                  pl.BlockSpec((1, G) + win.shape[2:], lambda b, i: (b, 0, 0, 0)),
                  pl.BlockSpec(mmat.shape, lambda b, i: (0, 0)),
                  pl.BlockSpec(btail.shape, lambda b, i: (0, 0, 0, 0)),
                  pl.BlockSpec(bwin.shape, lambda b, i: (0, 0, 0))],
        out_specs=pl.BlockSpec((1, qb, 512), lambda b, i: (b, i, 0)),
        out_shape=jax.ShapeDtypeStruct((B, S, 512), BF16),
        compiler_params=pltpu.CompilerParams(dimension_semantics=("arbitrary", "arbitrary"),
                                             vmem_limit_bytes=VMEM_LIMIT),
    )(qn, gn, kvcmp, slc, win, mmat, btail, bwin)


def _swa_kernel(qs_ref, kv_ref, bias_ref, sink_ref, o_ref):
    qi = pl.program_id(1)
    s0 = qi * SWA_Q_BLOCK
    hg = SWA_HEADS // SWA_KV_HEADS
    q = qs_ref[0]
    var = jnp.minimum(qi, 1)
    groups = range(SWA_KV_HEADS)

    def group_rows(g):
        qg = jnp.concatenate([q[:, (g * hg + h) * HEAD_DIM:(g * hg + h + 1) * HEAD_DIM] for h in range(hg)],
                             axis=0)
        return jnp.concatenate([qg, jnp.zeros_like(qg)], axis=1)

    qg2 = [group_rows(g) for g in groups]
    kv0 = pl.multiple_of(KV_FRONT - SWA_WINDOW + s0, SWA_Q_BLOCK)
    kv = [kv_ref[0, g, pl.ds(kv0, SWA_KEYS), :] for g in groups]
    s = [(_dot_nt(qg2[g], kv[g]) + bias_ref[var, g]).astype(BF16) for g in groups]
    sink = [sink_ref[g] for g in groups]
    m = [jnp.maximum(jnp.max(s[g], axis=1, keepdims=True).astype(F32), sink[g]).astype(BF16) for g in groups]
    p = [jnp.exp(s[g] - m[g]) for g in groups]
    l = [jnp.sum(p[g].astype(F32), axis=1, keepdims=True) + jnp.exp(sink[g] - m[g].astype(F32)) for g in groups]
    o = [_dot(p[g], kv[g])[:, HEAD_DIM:] / l[g] for g in groups]
    for g in groups:
        for h in range(hg):
            head = g * hg + h
            o_ref[0, :, head * HEAD_DIM:(head + 1) * HEAD_DIM] = (
                o[g][h * SWA_Q_BLOCK:(h + 1) * SWA_Q_BLOCK].astype(BF16))


def _swa(qs, kvp, bias, sink_rows):
    B, S, _ = qs.shape
    qb = SWA_Q_BLOCK
    return pl.pallas_call(
        _swa_kernel,
        grid=(B, S // qb),
        in_specs=[pl.BlockSpec((1, qb, 512), lambda b, i: (b, i, 0)),
                  pl.BlockSpec((1, SWA_KV_HEADS) + kvp.shape[2:], lambda b, i: (b, 0, 0, 0)),
                  pl.BlockSpec(bias.shape, lambda b, i: (0, 0, 0, 0)),
                  pl.BlockSpec(sink_rows.shape, lambda b, i: (0, 0, 0))],
        out_specs=pl.BlockSpec((1, qb, 512), lambda b, i: (b, i, 0)),
        out_shape=jax.ShapeDtypeStruct((B, S, 512), BF16),
        compiler_params=pltpu.CompilerParams(dimension_semantics=("arbitrary", "arbitrary"),
                                             vmem_limit_bytes=VMEM_LIMIT),
    )(qs, kvp, bias, sink_rows)


def _pack_bf16_pair(lo, hi):
    lo_b = pltpu.bitcast(lo.astype(BF16).astype(F32), jnp.uint32)
    hi_b = pltpu.bitcast(hi.astype(BF16).astype(F32), jnp.uint32)
    return (lo_b >> 16) | (hi_b & jnp.uint32(0xFFFF0000))


def _unpack_bf16_pair(w):
    lo = pltpu.bitcast(w << 16, F32).astype(BF16)
    hi = pltpu.bitcast(w & jnp.uint32(0xFFFF0000), F32).astype(BF16)
    return lo, hi


def _merge_kernel(on_ref, os_ref, gm_ref, x_ref, mod_ref, wb_ref, wo_ref, g_ref, wr_ref, br_ref, ltri_ref,
                  xo_ref, hp_ref, idx_ref, gate_ref, cnt_ref, run_ref):
    d = D_MODEL
    a = _dot(on_ref[...], wb_ref[0])
    b = _dot(os_ref[...], wb_ref[1])
    gm = gm_ref[...]
    merged = _sigmoid(gm[:, :d]) * a + _sigmoid(gm[:, d:]) * b
    y = _dot(merged.astype(BF16), wo_ref[...])
    xn = x_ref[...] + mod_ref[0, 2:3, :] * y
    xo_ref[...] = xn
    ms = jnp.mean(xn * xn, axis=-1, keepdims=True)
    h2 = xn * lax.rsqrt(ms + RMS_EPS) * g_ref[...]
    h2 = h2 * (1.0 + mod_ref[0, 4:5, :]) + mod_ref[0, 3:4, :]
    hp_ref[...] = _pack_bf16_pair(h2[:, :d // 2], h2[:, d // 2:])
    h_hi, h_mid, h_lo = _split3(h2)
    w_hi, w_mid, w_lo = wr_ref[0], wr_ref[1], wr_ref[2]
    logits = (_dot(h_hi, w_hi) + (_dot(h_hi, w_mid) + _dot(h_mid, w_hi))
              + (_dot(h_hi, w_lo) + _dot(h_mid, w_mid) + _dot(h_lo, w_hi))) + br_ref[...]
    tm = logits.shape[0]
    lane = lax.broadcasted_iota(jnp.int32, (tm, LANES), 1)
    c = jnp.where(lane < N_EXPERTS, logits, -jnp.inf)
    vals, idxs = [], []
    for _ in range(TOP_K):
        mx = jnp.max(c, axis=1, keepdims=True)
        first = jnp.min(jnp.where(c == mx, lane, LANES), axis=1, keepdims=True)
        vals.append(mx)
        idxs.append(first)
        c = jnp.where(lane == first, -jnp.inf, c)
    es = [jnp.exp(v - vals[0]) for v in vals]
    den = es[0] + es[1] + es[2] + es[3]

    @pl.when(pl.program_id(0) == 0)
    def _():
        run_ref[...] = jnp.zeros_like(run_ref)

    onehot = jnp.zeros((tm, LANES), F32)
    for k in range(TOP_K):
        onehot = jnp.where(lane == idxs[k], 1.0, onehot)
    before = _dot(ltri_ref[...], onehot.astype(BF16)) + run_ref[...]
    run_ref[...] = run_ref[...] + jnp.sum(onehot, axis=0, keepdims=True)
    cnt_ref[...] = run_ref[...].astype(jnp.int32)

    idx_out = jnp.zeros((tm, LANES), jnp.int32)
    gate_out = jnp.zeros((tm, LANES), F32)
    for k in range(TOP_K):
        rank = jnp.sum(jnp.where(lane == idxs[k], before, 0.0), axis=1, keepdims=True).astype(jnp.int32)
        idx_out = jnp.where(lane == k, idxs[k], idx_out)
        idx_out = jnp.where(lane == TOP_K + k, rank, idx_out)
        gate_out = jnp.where(lane == k, es[k] / den, gate_out)
    idx_ref[...] = idx_out
    gate_ref[...] = gate_out


def _merge(on, os_, gm, x, mod, wb, wo, g_ffn, wr3, br, S):
    T, D = x.shape
    tm = 256
    nb_per_seq = S // tm
    ltri = jnp.asarray(np.tril(np.ones((tm, tm), np.float32), -1), BF16)
    return pl.pallas_call(
        _merge_kernel,
        grid=(T // tm,),
        in_specs=[pl.BlockSpec((tm, 512), lambda i: (i, 0)),
                  pl.BlockSpec((tm, 512), lambda i: (i, 0)),
                  pl.BlockSpec((tm, 2 * D), lambda i: (i, 0)),
                  pl.BlockSpec((tm, D), lambda i: (i, 0)),
                  pl.BlockSpec((1, 6, D), lambda i: (i // nb_per_seq, 0, 0)),
                  pl.BlockSpec((2, 512, D), lambda i: (0, 0, 0)),
                  pl.BlockSpec((D, D), lambda i: (0, 0)),
                  pl.BlockSpec((1, D), lambda i: (0, 0)),
                  pl.BlockSpec((3, D, LANES), lambda i: (0, 0, 0)),
                  pl.BlockSpec((1, LANES), lambda i: (0, 0)),
                  pl.BlockSpec((tm, tm), lambda i: (0, 0))],
        out_specs=[pl.BlockSpec((tm, D), lambda i: (i, 0)),
                   pl.BlockSpec((tm, D // 2), lambda i: (i, 0)),
                   pl.BlockSpec((tm, LANES), lambda i: (i, 0)),
                   pl.BlockSpec((tm, LANES), lambda i: (i, 0)),
                   pl.BlockSpec((1, LANES), lambda i: (0, 0))],
        out_shape=[jax.ShapeDtypeStruct((T, D), F32),
                   jax.ShapeDtypeStruct((T, D // 2), jnp.uint32),
                   jax.ShapeDtypeStruct((T, LANES), jnp.int32),
                   jax.ShapeDtypeStruct((T, LANES), F32),
                   jax.ShapeDtypeStruct((1, LANES), jnp.int32)],
        scratch_shapes=[pltpu.VMEM((1, LANES), F32)],
        compiler_params=pltpu.CompilerParams(dimension_semantics=("arbitrary",),
                                             vmem_limit_bytes=VMEM_LIMIT),
    )(on, os_, gm, x, mod, wb, wo, g_ffn, wr3, br, ltri)


def _dispatch_kernel(dest_ref, hp_ref, xs_in_hbm, xs_hbm, sem):
    del xs_in_hbm
    tm = hp_ref.shape[0]

    def body(r, _):
        for k in range(TOP_K):
            pltpu.make_async_copy(hp_ref.at[pl.ds(r, 1), :],
                                  xs_hbm.at[pl.ds(dest_ref[0, 0, r * TOP_K + k], 1), :], sem).start()
        return 0
    lax.fori_loop(0, tm, body, 0, unroll=DMA_LOOP_UNROLL)
    for _ in range(TOP_K):
        pltpu.make_async_copy(hp_ref, xs_hbm.at[pl.ds(0, tm), :], sem).wait()


def _dispatch(dest, hp, n_slots):
    T, W = hp.shape
    tm = 256
    xs0 = jnp.zeros((n_slots, W), hp.dtype)
    return pl.pallas_call(
        _dispatch_kernel,
        grid=(T // tm,),
        in_specs=[pl.BlockSpec((1, 1, tm * TOP_K), lambda i: (i, 0, 0), memory_space=pltpu.SMEM),
                  pl.BlockSpec((tm, W), lambda i: (i, 0)),
                  pl.BlockSpec(memory_space=pl.ANY)],
        out_specs=pl.BlockSpec(memory_space=pl.ANY),
        out_shape=jax.ShapeDtypeStruct((n_slots, W), hp.dtype),
        scratch_shapes=[pltpu.SemaphoreType.DMA(())],
        input_output_aliases={2: 0},
        compiler_params=pltpu.CompilerParams(dimension_semantics=("arbitrary",),
                                             vmem_limit_bytes=VMEM_LIMIT),
    )(dest.reshape(T // tm, 1, tm * TOP_K), hp, xs0)


def _moe_kernel(bexp_ref, used_ref, xs_ref, wu_ref, wd_ref, bu_ref, bd_ref, ys_ref):
    i = pl.program_id(0)

    @pl.when(used_ref[i] > 0)
    def _():
        x_lo, x_hi = _unpack_bf16_pair(xs_ref[...])
        half = D_MODEL // 2
        hid = []
        for t in range(2 * D_EXPERT // UP_TILE):
            cols = slice(t * UP_TILE, (t + 1) * UP_TILE)
            gu = _dot(x_lo, wu_ref[0, :half, cols]) + _dot(x_hi, wu_ref[0, half:, cols]) + bu_ref[0, :, cols]
            glu = jnp.minimum(gu[:, :UP_TILE // 2], SWIGLU_LIMIT)
            lin = jnp.clip(gu[:, UP_TILE // 2:], -SWIGLU_LIMIT, SWIGLU_LIMIT)
            hid.append((glu * _sigmoid(SWIGLU_ALPHA * glu) * (lin + 1.0)).astype(BF16))
        ys_ref[...] = _dot(jnp.concatenate(hid, axis=1), wd_ref[0]) + bd_ref[0]

    @pl.when(used_ref[i] == 0)
    def _():
        ys_ref[...] = jnp.zeros_like(ys_ref)


def _wup_prep_kernel(w_ref, p_ref, o_ref):
    for t in range(w_ref.shape[2] // UP_TILE):
        cols = slice(t * UP_TILE, (t + 1) * UP_TILE)
        o_ref[0, :, cols] = _dot(w_ref[0, :, cols].astype(BF16), p_ref[...]).astype(BF16)


def _wup_prep(w_up):
    N, D, W = w_up.shape
    j = np.arange(UP_TILE)
    perm = np.zeros((UP_TILE, UP_TILE), np.float32)
    perm[j, (j % 2) * (UP_TILE // 2) + j // 2] = 1.0
    return pl.pallas_call(
        _wup_prep_kernel,
        grid=(N,),
        in_specs=[pl.BlockSpec((1, D, W), lambda n: (n, 0, 0)),
                  pl.BlockSpec((UP_TILE, UP_TILE), lambda n: (0, 0))],
        out_specs=pl.BlockSpec((1, D, W), lambda n: (n, 0, 0)),
        out_shape=jax.ShapeDtypeStruct((N, D, W), BF16),
        compiler_params=pltpu.CompilerParams(dimension_semantics=("arbitrary",),
                                             vmem_limit_bytes=VMEM_LIMIT),
    )(w_up, jnp.asarray(perm, BF16))


def _moe(bexp, used, xs, wu, wd, bu, bd):
    nblk = bexp.shape[0]
    D = D_MODEL
    tm = MOE_BLOCK
    grid_spec = pltpu.PrefetchScalarGridSpec(
        num_scalar_prefetch=2,
        grid=(nblk,),
        in_specs=[pl.BlockSpec((tm, D // 2), lambda i, be, us: (i, 0)),
                  pl.BlockSpec((1, D, 2 * D_EXPERT), lambda i, be, us: (be[i], 0, 0)),
                  pl.BlockSpec((1, D_EXPERT, D), lambda i, be, us: (be[i], 0, 0)),
                  pl.BlockSpec((1, 1, 2 * D_EXPERT), lambda i, be, us: (be[i], 0, 0)),
                  pl.BlockSpec((1, 1, D), lambda i, be, us: (be[i], 0, 0))],
        out_specs=pl.BlockSpec((tm, D), lambda i, be, us: (i, 0)),
    )
    return pl.pallas_call(
        _moe_kernel,
        grid_spec=grid_spec,
        out_shape=jax.ShapeDtypeStruct((nblk * tm, D), F32),
        compiler_params=pltpu.CompilerParams(dimension_semantics=("arbitrary",),
                                             vmem_limit_bytes=VMEM_LIMIT),
    )(bexp, used, xs, wu, wd, bu, bd)


def _combine_kernel(dest_ref, destn_ref, x_ref, gate_ref, mod_ref, g_ref, ys_hbm, o_ref, ybuf, sem, *, final):
    i = pl.program_id(0)
    nt = pl.num_programs(0)
    slot = i % 2
    tm = x_ref.shape[0]

    def issue(ids_ref, sl):
        def body(r, _):
            for k in range(TOP_K):
                pltpu.make_async_copy(ys_hbm.at[pl.ds(ids_ref[0, 0, r * TOP_K + k], 1), :],
                                      ybuf.at[sl, k, pl.ds(r, 1), :], sem.at[sl]).start()
            return 0
        lax.fori_loop(0, tm, body, 0, unroll=DMA_LOOP_UNROLL)

    @pl.when(i == 0)
    def _():
        issue(dest_ref, 0)

    @pl.when(i + 1 < nt)
    def _():
        issue(destn_ref, 1 - slot)

    for k in range(TOP_K):
        pltpu.make_async_copy(ys_hbm.at[pl.ds(0, tm), :], ybuf.at[slot, k], sem.at[slot]).wait()

    gate = gate_ref[...]
    y = gate[:, 0:1] * ybuf[slot, 0]
    for k in range(1, TOP_K):
        y = y + gate[:, k:k + 1] * ybuf[slot, k]
    xn = x_ref[...] + mod_ref[0, 5:6, :] * y
    if final:
        ms = jnp.mean(xn * xn, axis=-1, keepdims=True)
        xn = xn * lax.rsqrt(ms + RMS_EPS) * g_ref[...]
    o_ref[...] = xn


def _combine(dest, x, ys, gate, mod, g_final, S, final):
    T, D = x.shape
    tm = 256
    nb_per_seq = S // tm
    nt = T // tm
    dest3 = dest.reshape(nt, 1, tm * TOP_K)
    smem_blk = lambda f: pl.BlockSpec((1, 1, tm * TOP_K), f, memory_space=pltpu.SMEM)
    return pl.pallas_call(
        functools.partial(_combine_kernel, final=final),
        grid=(nt,),
        in_specs=[smem_blk(lambda i: (i, 0, 0)),
                  smem_blk(lambda i: (jnp.minimum(i + 1, nt - 1), 0, 0)),
                  pl.BlockSpec((tm, D), lambda i: (i, 0)),
                  pl.BlockSpec((tm, LANES), lambda i: (i, 0)),
                  pl.BlockSpec((1, 6, D), lambda i: (i // nb_per_seq, 0, 0)),
                  pl.BlockSpec((1, D), lambda i: (0, 0)),
                  pl.BlockSpec(memory_space=pl.ANY)],
        out_specs=pl.BlockSpec((tm, D), lambda i: (i, 0)),
        out_shape=jax.ShapeDtypeStruct((T, D), F32),
        scratch_shapes=[pltpu.VMEM((2, TOP_K, tm, D), F32),
                        pltpu.SemaphoreType.DMA((2,))],
        compiler_params=pltpu.CompilerParams(dimension_semantics=("arbitrary",),
                                             vmem_limit_bytes=VMEM_LIMIT),
    )(dest3, dest3, x, gate, mod, g_final, ys)


def _t5_bucket(dist):
    n = jnp.maximum(dist, 0)
    max_exact = REL_BUCKETS // 2
    log_ratio = jnp.log(jnp.maximum(n, 1).astype(F32) / max_exact) / math.log(REL_MAX_DIST / max_exact)
    large = max_exact + (log_ratio * (REL_BUCKETS - max_exact)).astype(jnp.int32)
    return jnp.where(n < max_exact, n, jnp.minimum(large, REL_BUCKETS - 1))


def _toeplitz_bias(tab, q_block, n_keys, back, window, variants, shift_far):
    r = jnp.arange(q_block)[:, None]
    c = jnp.arange(n_keys)[None, :]
    dist = r + back - c
    ok = dist >= 0
    if window is not None:
        ok = ok & (dist < window)
    in_bucket = _t5_bucket(dist)[..., None] == jnp.arange(REL_BUCKETS)
    bias = jnp.sum(jnp.where(in_bucket[..., None, None], tab, 0.0), axis=2)
    if shift_far:
        bias = bias - tab[REL_BUCKETS - 1][None, None]
    bias = jnp.transpose(bias, (2, 3, 0, 1))
    outs = []
    for v in range(variants):
        okv = ok if v == variants - 1 else ok & (c >= back - v * q_block)
        outs.append(jnp.where(okv[None, None], bias, NEG))
    out = jnp.stack(outs)
    V, G, Hg = out.shape[:3]
    return out.reshape(V, G, Hg * q_block, n_keys).astype(F32)


def _routing_tables(route, counts, T):
    E = N_EXPERTS
    n_blocks = -(-(T * TOP_K) // MOE_BLOCK) + E
    padded = (counts + MOE_BLOCK - 1) // MOE_BLOCK * MOE_BLOCK
    pad_end = jnp.cumsum(padded)
    grp_start = pad_end - padded
    e_idx = route[:, :TOP_K]
    onehot = e_idx[..., None] == jnp.arange(E, dtype=jnp.int32)
    dest = jnp.sum(jnp.where(onehot, grp_start, 0), axis=-1) + route[:, TOP_K:2 * TOP_K]
    blk_start = jnp.arange(n_blocks, dtype=jnp.int32) * MOE_BLOCK
    bexp = jnp.minimum(jnp.sum(pad_end[None, :] <= blk_start[:, None], axis=1), E - 1).astype(jnp.int32)
    used = (blk_start < pad_end[-1]).astype(jnp.int32)
    return dest.astype(jnp.int32), bexp, used


def kernel(x, c, w_ada, b_ada, g_mix, g_ffn, g_final, w_in, cmp_pos, cmp_w1, cmp_w2, sinks, rel_tab,
           w_branch, w_out, w_router, b_router, w_up, b_up, w_down, b_down):
    B, S, D = x.shape
    L = w_ada.shape[0]
    T = B * S
    G = NSA_GROUPS
    nb = S // SEL_BLOCK
    ncp = S // CMP_STRIDE

    w_in_p = _reorder_in_proj(w_in)
    pos_r = cmp_pos.reshape(L, 2, 2, 1, CMP_STRIDE * HEAD_DIM)
    w1_r = cmp_w1.reshape(L, 2, 2, CMP_STRIDE * HEAD_DIM, CMP_HIDDEN).astype(BF16)
    w2_r = cmp_w2.astype(BF16)
    wb = w_branch.astype(BF16)
    wo = w_out.astype(BF16)
    wr_pad = jnp.pad(w_router, ((0, 0), (0, 0), (0, LANES - N_EXPERTS)))
    wr_hi = wr_pad.astype(BF16)
    wr_r1 = wr_pad - wr_hi.astype(F32)
    wr_mid = wr_r1.astype(BF16)
    wr_lo = (wr_r1 - wr_mid.astype(F32)).astype(BF16)
    wr3 = jnp.stack([wr_hi, wr_mid, wr_lo], axis=1)
    br = jnp.pad(b_router, ((0, 0), (0, LANES - N_EXPERTS))).reshape(L, 1, LANES)
    wu = _wup_prep(w_up.reshape(L * N_EXPERTS, D, 2 * D_EXPERT))
    n_up_tiles = 2 * D_EXPERT // UP_TILE
    bu = b_up.reshape(L * N_EXPERTS, n_up_tiles, UP_TILE // 2, 2).transpose(0, 1, 3, 2).reshape(
        L * N_EXPERTS, 1, 2 * D_EXPERT)
    wd = w_down.reshape(L * N_EXPERTS, D_EXPERT, D).astype(BF16)
    bd = b_down.reshape(L * N_EXPERTS, 1, D)

    nsa_tab = rel_tab[:, :NSA_HEADS].reshape(REL_BUCKETS, G, HEADS_PER_GROUP)
    swa_tab = rel_tab[:, NSA_HEADS:].reshape(REL_BUCKETS, SWA_KV_HEADS, SWA_HEADS // SWA_KV_HEADS)
    btail = _toeplitz_bias(nsa_tab, NSA_Q_BLOCK, TAIL_KEYS, SEL_NEAR, None, 3, True)
    bwin = _toeplitz_bias(nsa_tab, NSA_Q_BLOCK, WIN_KEYS, NSA_WINDOW, NSA_WINDOW, 1, False)[0]
    bswa = _toeplitz_bias(swa_tab, SWA_Q_BLOCK, SWA_KEYS, SWA_WINDOW, SWA_WINDOW, 2, False)
    ci = jnp.arange(ncp)[:, None]
    bj = jnp.arange(nb)[None, :]
    delta = (SEL_BLOCK // CMP_STRIDE) * bj - ci
    mmat = jnp.where((delta == 0) | (delta == 4), 1.0, jnp.where((delta >= 1) & (delta <= 3), 2.0, 0.0))
    mmat = jnp.where(ci < ncp - 1, mmat, 0.0).astype(BF16)
    slc_rows = KV_FRONT + S
    kpos = jnp.arange(slc_rows)[:, None] - KV_FRONT
    onehot_blk = ((kpos >= 0) & (kpos < S) & (kpos // SEL_BLOCK == jnp.arange(nb)[None, :])).astype(BF16)
    slc_t = jnp.broadcast_to(
        jnp.concatenate([jnp.zeros((slc_rows, 2 * HEAD_DIM), BF16), onehot_blk], axis=1),
        (B, G, slc_rows, 2 * HEAD_DIM + nb))
    win_t = jnp.zeros((B, G, KV_FRONT + S, 2 * HEAD_DIM), BF16)

    c_pad = jnp.pad(c, ((0, 8 - B), (0, 0)))
    mod_all = _ada_mod(c_pad, w_ada, b_ada)[:, :B].reshape(L, B, 6, D)

    xf = x
    for l in range(L):
        mod = mod_all[l]
        qn, kvc, slc, win, gn, qs, swa_kv, gm = _inproj(xf, mod, g_mix[l].reshape(1, D), w_in_p[l],
                                                        slc_t, win_t, win_t)
        xc = kvc.reshape(B, 2 * G, ncp, CMP_STRIDE * HEAD_DIM)
        kvcmp = _compress(xc, pos_r[l], w1_r[l], w2_r[l])
        o_n = _nsa(qn, gn, kvcmp, slc, win, mmat, btail, bwin)
        sink_rows = jnp.repeat(sinks[l].reshape(SWA_KV_HEADS, SWA_HEADS // SWA_KV_HEADS), SWA_Q_BLOCK,
                               axis=1)[..., None]
        o_s = _swa(qs, swa_kv, bswa, sink_rows)
        x2, hp, route, gate, counts = _merge(o_n.reshape(T, 512), o_s.reshape(T, 512), gm.reshape(T, 2 * D),
                                             xf.reshape(T, D), mod, wb[l], wo[l], g_ffn[l].reshape(1, D),
                                             wr3[l], br[l], S)
        dest, bexp, used = _routing_tables(route, counts[0, :N_EXPERTS], T)
        xs = _dispatch(dest, hp, bexp.shape[0] * MOE_BLOCK)
        ys = _moe(bexp + l * N_EXPERTS, used, xs, wu, wd, bu, bd)
        xf = _combine(dest, x2, ys, gate, mod, g_final.reshape(1, D), S,
                      final=(l == L - 1)).reshape(B, S, D)
    return xf
```

```python
import functools
import math

import numpy as np
import jax
import jax.numpy as jnp
from jax import lax
from jax.experimental import pallas as pl
from jax.experimental.pallas import tpu as pltpu

D_MODEL = 1024
HEAD_DIM = 64
NSA_HEADS = 8
NSA_GROUPS = 2
HEADS_PER_GROUP = NSA_HEADS // NSA_GROUPS
CMP_LEN = 32
CMP_STRIDE = 16
CMP_HIDDEN = 256
SEL_BLOCK = 64
SEL_TOP_N = 16
NSA_WINDOW = 512
NSA_Q_BLOCK = 64
SWA_HEADS = 8
SWA_KV_HEADS = 2
SWA_WINDOW = 128
SWA_Q_BLOCK = 128
REL_BUCKETS = 32
REL_MAX_DIST = 128
N_EXPERTS = 32
TOP_K = 4
D_EXPERT = 1024
SWIGLU_ALPHA = 1.702
SWIGLU_LIMIT = 7.0
MOE_BLOCK = 256
RMS_EPS = 1e-5

LANES = 128
NEG = -1e30
FAR_TILE = 1024
FAR_TILE_BLOCKS = FAR_TILE // SEL_BLOCK
SEL_NEAR = 2 * SEL_BLOCK
TAIL_KEYS = SEL_NEAR + NSA_Q_BLOCK
WIN_KEYS = NSA_WINDOW + NSA_Q_BLOCK
SWA_KEYS = SWA_WINDOW + SWA_Q_BLOCK
KV_FRONT = 512
N_FORCED = 3
UP_TILE = 256
SUBLANES = 8
SWA_BLOCKS_PER_STEP = 4
MERGE_PARTS = 2
VMEM_LIMIT = 56 * 1024 * 1024

F32 = jnp.float32
BF16 = jnp.bfloat16

_OFF_QN = 0
_OFF_KVN = 512
_OFF_GN = 1280
_OFF_QS = 1304
_OFF_KVS = 1816
_OFF_GM = 2072
_D_IN = 4120
_P_QN, _P_CMP, _P_SLC, _P_WIN, _P_GN, _P_QS, _P_SWA, _P_GM, _P_END = (
    0, 512, 768, 1024, 1280, 1408, 1920, 2176, 4224)


def _sigmoid(v):
    return 1.0 / (1.0 + jnp.exp(-v))


def _dot(a, b):
    return jnp.dot(a, b, preferred_element_type=F32)


def _dot_nt(a, b):
    return lax.dot_general(a, b, (((1,), (1,)), ((), ())), preferred_element_type=F32)


def _split3(v):
    hi = v.astype(BF16)
    r1 = v - hi.astype(F32)
    mid = r1.astype(BF16)
    lo = (r1 - mid.astype(F32)).astype(BF16)
    return hi, mid, lo


def _reorder_in_proj(w_in):
    def cols(a, n):
        return w_in[..., a:a + n]

    def kv_pairs(base):
        return [cols(base + off, 64) for off in (0, 128, 64, 192)]

    parts = [cols(_OFF_QN, 512)]
    parts += kv_pairs(_OFF_KVN)
    parts += kv_pairs(_OFF_KVN + 256)
    parts += kv_pairs(_OFF_KVN + 512)
    parts += [cols(_OFF_GN, 24), jnp.zeros(w_in.shape[:-1] + (LANES - 24,), w_in.dtype)]
    parts += [cols(_OFF_QS, 512)]
    parts += kv_pairs(_OFF_KVS)
    parts += [cols(_OFF_GM, 2048)]
    out = jnp.concatenate(parts, axis=-1).astype(BF16)
    assert out.shape[-1] == _P_END
    return out


def _ada_kernel(c_ref, w_ref, b_ref, o_ref):
    c = c_ref[...]
    cond = c * _sigmoid(c)
    o_ref[0] = _dot(cond.astype(BF16), w_ref[0].astype(BF16)) + b_ref[0]


def _ada_mod(c_pad, w_ada, b_ada):
    L, D, N = w_ada.shape
    tn = 1536
    return pl.pallas_call(
        _ada_kernel,
        grid=(L, N // tn),
        in_specs=[pl.BlockSpec((8, D), lambda l, j: (0, 0)),
                  pl.BlockSpec((1, D, tn), lambda l, j: (l, 0, j)),
                  pl.BlockSpec((1, 1, tn), lambda l, j: (l, 0, j))],
        out_specs=pl.BlockSpec((1, 8, tn), lambda l, j: (l, 0, j)),
        out_shape=jax.ShapeDtypeStruct((L, 8, N), F32),
        compiler_params=pltpu.CompilerParams(dimension_semantics=("arbitrary", "arbitrary"),
                                             vmem_limit_bytes=VMEM_LIMIT),
    )(c_pad, w_ada, b_ada.reshape(L, 1, N))


def _inproj_kernel(x_ref, mod_ref, g_ref, w_ref, slc_in, win_in, swa_in,
                   qn_ref, kvc_ref, slc_ref, win_ref, gn_ref, qs_ref, swa_ref, gm_ref):
    del slc_in, win_in, swa_in
    x = x_ref[0]
    ms = jnp.mean(x * x, axis=-1, keepdims=True)
    y = x * lax.rsqrt(ms + RMS_EPS) * g_ref[...]
    h = (y * (1.0 + mod_ref[0, 1:2, :]) + mod_ref[0, 0:1, :]).astype(BF16)

    def proj(a, b):
        return _dot(h, w_ref[:, a:b])

    def per_group(ref, vals, width):
        for c in range(vals.shape[1] // width):
            ref[0, c] = vals[:, c * width:(c + 1) * width].astype(ref.dtype)

    scale = HEAD_DIM ** -0.5
    qn_ref[0] = (proj(_P_QN, _P_CMP) * scale).astype(BF16)
    per_group(kvc_ref, proj(_P_CMP, _P_SLC), HEAD_DIM)
    per_group(slc_ref, proj(_P_SLC, _P_WIN), 2 * HEAD_DIM)
    per_group(win_ref, proj(_P_WIN, _P_GN), 2 * HEAD_DIM)
    gn_ref[0] = proj(_P_GN, _P_QS)
    qs_ref[0] = (proj(_P_QS, _P_SWA) * scale).astype(BF16)
    per_group(swa_ref, proj(_P_SWA, _P_GM), 2 * HEAD_DIM)
    gm_ref[0] = proj(_P_GM, _P_END)


def _inproj(x, mod, g, w, slc_t, win_t, swa_t):
    B, S, D = x.shape
    G = NSA_GROUPS
    tm = KV_FRONT
    front = KV_FRONT // tm
    plain = lambda n: pl.BlockSpec((1, tm, n), lambda b, i: (b, i, 0))
    padded = pl.BlockSpec((1, G, tm, 2 * HEAD_DIM), lambda b, i: (b, 0, i + front, 0))
    hbm = pl.BlockSpec(memory_space=pl.ANY)
    sds = jax.ShapeDtypeStruct
    return pl.pallas_call(
        _inproj_kernel,
        grid=(B, S // tm),
        in_specs=[pl.BlockSpec((1, tm, D), lambda b, i: (b, i, 0)),
                  pl.BlockSpec((1, 6, D), lambda b, i: (b, 0, 0)),
                  pl.BlockSpec((1, D), lambda b, i: (0, 0)),
                  pl.BlockSpec((D, _P_END), lambda b, i: (0, 0)),
                  hbm, hbm, hbm],
        out_specs=[plain(512),
                   pl.BlockSpec((1, 2 * G, tm, HEAD_DIM), lambda b, i: (b, 0, i, 0)),
                   padded, padded, plain(LANES), plain(512), padded, plain(2 * D)],
        out_shape=[sds((B, S, 512), BF16), sds((B, 2 * G, S, HEAD_DIM), F32),
                   sds(slc_t.shape, BF16), sds(win_t.shape, BF16), sds((B, S, LANES), F32),
                   sds((B, S, 512), BF16), sds(swa_t.shape, BF16), sds((B, S, 2 * D), F32)],
        input_output_aliases={4: 2, 5: 3, 6: 6},
        compiler_params=pltpu.CompilerParams(dimension_semantics=("arbitrary", "arbitrary"),
                                             vmem_limit_bytes=VMEM_LIMIT),
    )(x, mod, g, w, slc_t, win_t, swa_t)


def _gelu_tanh(v):
    return 0.5 * v * (1.0 + jnp.tanh(math.sqrt(2.0 / math.pi) * (v + 0.044715 * (v * v * v))))


def _compress_kernel(x_ref, pos_ref, w1_ref, w2_ref, o_ref):
    nch = x_ref.shape[2]
    row = lax.broadcasted_iota(jnp.int32, (nch, 1), 0)
    outs = []
    for kv in range(2):
        xc = x_ref[0, kv]
        a = _dot((xc + pos_ref[kv, 0]).astype(BF16), w1_ref[kv, 0])
        b = _dot((xc + pos_ref[kv, 1]).astype(BF16), w1_ref[kv, 1])
        b_next = pltpu.roll(b, nch - 1, axis=0)
        pre = a + jnp.where(row < nch - 1, b_next, 0.0)
        outs.append(_dot(_gelu_tanh(pre).astype(BF16), w2_ref[kv]))
    o_ref[0, 0] = jnp.concatenate(outs, axis=-1).astype(BF16)


def _compress(xc, pos, w1, w2):
    B, _, nch, cw = xc.shape
    return pl.pallas_call(
        _compress_kernel,
        grid=(B, NSA_GROUPS),
        in_specs=[pl.BlockSpec((1, 2, nch, cw), lambda b, g: (b, g, 0, 0)),
                  pl.BlockSpec((2, 2, 1, cw), lambda b, g: (0, 0, 0, 0)),
                  pl.BlockSpec((2, 2, cw, CMP_HIDDEN), lambda b, g: (0, 0, 0, 0)),
                  pl.BlockSpec((2, CMP_HIDDEN, HEAD_DIM), lambda b, g: (0, 0, 0))],
        out_specs=pl.BlockSpec((1, 1, nch, 2 * HEAD_DIM), lambda b, g: (b, g, 0, 0)),
        out_shape=jax.ShapeDtypeStruct((B, NSA_GROUPS, nch, 2 * HEAD_DIM), BF16),
        compiler_params=pltpu.CompilerParams(dimension_semantics=("arbitrary", "arbitrary"),
                                             vmem_limit_bytes=VMEM_LIMIT),
    )(xc, pos, w1, w2)


def _nsa_kernel(qn_ref, gn_ref, cmp_ref, slc_ref, win_ref, mmat_ref, btail_ref, bwin_ref, o_ref):
    qi = pl.program_id(1)
    s0 = qi * NSA_Q_BLOCK
    blk_t = qi
    ncp = cmp_ref.shape[2]
    nb = mmat_ref.shape[1]
    rows = HEADS_PER_GROUP * NSA_Q_BLOCK
    q = qn_ref[0]
    gates = _sigmoid(gn_ref[0])
    var = jnp.minimum(qi, 2)

    def stack_heads(v):
        return jnp.concatenate([v] * HEADS_PER_GROUP, axis=0)

    groups = range(NSA_GROUPS)
    col_c = lax.broadcasted_iota(jnp.int32, (NSA_Q_BLOCK, ncp), 1)
    t_c = s0 + lax.broadcasted_iota(jnp.int32, (NSA_Q_BLOCK, ncp), 0)
    bias_c = stack_heads(jnp.where(col_c * CMP_STRIDE + (CMP_LEN - 1) <= t_c, 0.0, NEG))
    col_w = lax.broadcasted_iota(jnp.int32, (1, WIN_KEYS), 1)
    front_w = jnp.where(col_w >= NSA_WINDOW - s0, 0.0, NEG)
    def group_rows(g):
        qg = jnp.concatenate(
            [q[:, (g * HEADS_PER_GROUP + h) * HEAD_DIM:(g * HEADS_PER_GROUP + h + 1) * HEAD_DIM]
             for h in range(HEADS_PER_GROUP)], axis=0)
        return jnp.concatenate([qg, jnp.zeros_like(qg)], axis=1)

    qg2 = [group_rows(g) for g in groups]

    kvc = [cmp_ref[0, g] for g in groups]
    s_c = [_dot_nt(qg2[g], kvc[g]) + bias_c for g in groups]
    m_c = [jnp.max(s_c[g], axis=1, keepdims=True) for g in groups]
    e_c = [jnp.exp(s_c[g] - m_c[g]) for g in groups]
    d_c = [jnp.sum(e_c[g], axis=1, keepdims=True) for g in groups]
    p_c = [e_c[g] * jnp.where(m_c[g] > 0.5 * NEG, 1.0 / d_c[g], 0.0) for g in groups]
    o_c = [_dot(p_c[g].astype(BF16), kvc[g])[:, HEAD_DIM:] for g in groups]
    imp = [(p_c[g][0:NSA_Q_BLOCK] + p_c[g][NSA_Q_BLOCK:2 * NSA_Q_BLOCK]
            + p_c[g][2 * NSA_Q_BLOCK:3 * NSA_Q_BLOCK] + p_c[g][3 * NSA_Q_BLOCK:]) for g in groups]
    i_hi, i_mid, i_lo = _split3(jnp.concatenate(imp, axis=0))
    mm = mmat_ref[...]
    p_slc = _dot(i_hi, mm) + _dot(i_mid, mm) + _dot(i_lo, mm)

    win0 = pl.multiple_of(KV_FRONT - NSA_WINDOW + s0, NSA_Q_BLOCK)
    kvw = [win_ref[0, g, pl.ds(win0, WIN_KEYS), :] for g in groups]
    s_w = [(_dot_nt(qg2[g], kvw[g]) + bwin_ref[g] + front_w).astype(BF16) for g in groups]
    m_w = [jnp.max(s_w[g], axis=1, keepdims=True) for g in groups]
    p_w = [jnp.exp(s_w[g] - m_w[g]) for g in groups]
    l_w = [jnp.sum(p_w[g].astype(F32), axis=1, keepdims=True) for g in groups]
    o_w = [_dot(p_w[g], kvw[g])[:, HEAD_DIM:] / l_w[g] for g in groups]

    nq2 = NSA_GROUPS * NSA_Q_BLOCK
    p_t = p_slc.T
    blk = lax.broadcasted_iota(jnp.int32, (nb, nq2), 0)
    cand = (blk >= 1) & (blk <= blk_t - 2)
    c = jnp.where(cand, p_t, -jnp.inf)
    sel_t = jnp.zeros((nb, nq2), F32)
    for _ in range(SEL_TOP_N - N_FORCED):
        mx = jnp.max(c, axis=0, keepdims=True)
        first = jnp.min(jnp.where(c == mx, blk, nb), axis=0, keepdims=True)
        hit = blk == first
        sel_t = jnp.where(hit, 1.0, sel_t)
        c = jnp.where(hit, -jnp.inf, c)
    forced = (blk == 0) | (blk == blk_t) | (blk == blk_t - 1)
    all_causal = (blk <= blk_t) & (blk_t < SEL_TOP_N)
    sel_t = jnp.where(forced | all_causal, 1.0, sel_t)
    neg_tail_t = jnp.where(sel_t > 0, 0.0, NEG)
    neg_far_t = jnp.where(blk < blk_t - 2, neg_tail_t, NEG)
    neg_tail = neg_tail_t.T.astype(BF16)
    neg_far = neg_far_t.T.astype(BF16)
    q_far, q_tail = [], []
    for g in groups:
        r = slice(g * NSA_Q_BLOCK, (g + 1) * NSA_Q_BLOCK)
        q_far.append(jnp.concatenate([qg2[g], stack_heads(neg_far[r])], axis=1))
        q_tail.append(jnp.concatenate([qg2[g], stack_heads(neg_tail[r])], axis=1))

    def scores(q_aug, start, n_keys, bias):
        s = [_dot_nt(q_aug[g], slc_ref[0, g, pl.ds(start, n_keys), :]) for g in groups]
        if bias is not None:
            s = [s[g] + bias[g] for g in groups]
        return [s[g].astype(BF16) for g in groups]

    def update(carry, s, start, n_keys):
        kv = [slc_ref[0, g, pl.ds(start, n_keys), :] for g in groups]
        m_new = [jnp.maximum(carry[g][0], jnp.max(s[g], axis=1, keepdims=True).astype(F32)) for g in groups]
        p = [jnp.exp(s[g] - m_new[g].astype(BF16)) for g in groups]
        pv = [_dot(p[g], kv[g]) for g in groups]
        return tuple((m_new[g], jnp.exp(carry[g][0] - m_new[g]) * carry[g][1] + pv[g]) for g in groups)

    n_far = (jnp.maximum(blk_t - 2, 0) + FAR_TILE_BLOCKS - 1) // FAR_TILE_BLOCKS

    n_pairs = n_far // 2

    def far_pair(j, carry):
        a0 = pl.multiple_of(KV_FRONT + j * (2 * FAR_TILE), LANES)
        return update(carry, scores(q_far, a0, 2 * FAR_TILE, None), a0, 2 * FAR_TILE)

    def far_body(j, carry):
        a0 = pl.multiple_of(KV_FRONT + j * FAR_TILE, LANES)
        return update(carry, scores(q_far, a0, FAR_TILE, None), a0, FAR_TILE)

    init = (jnp.full((rows, 1), NEG, BF16).astype(F32), jnp.zeros((rows, 2 * HEAD_DIM + nb), F32))
    carry = lax.fori_loop(0, n_pairs, far_pair, (init,) * NSA_GROUPS)
    carry = lax.fori_loop(2 * n_pairs, n_far, far_body, carry)
    tail0 = pl.multiple_of(KV_FRONT - SEL_NEAR + s0, NSA_Q_BLOCK)
    carry = update(carry, scores(q_tail, tail0, TAIL_KEYS, [btail_ref[var, g] for g in groups]),
                   tail0, TAIL_KEYS)

    for g in groups:
        acc = carry[g][1]
        o_s = acc[:, HEAD_DIM:2 * HEAD_DIM] / jnp.sum(acc[:, 2 * HEAD_DIM:], axis=1, keepdims=True)
        for h in range(HEADS_PER_GROUP):
            head = g * HEADS_PER_GROUP + h
            r = slice(h * NSA_Q_BLOCK, (h + 1) * NSA_Q_BLOCK)
            o = (gates[:, head:head + 1] * o_c[g][r]
                 + gates[:, NSA_HEADS + head:NSA_HEADS + head + 1] * o_s[r]
                 + gates[:, 2 * NSA_HEADS + head:2 * NSA_HEADS + head + 1] * o_w[g][r])
            o_ref[0, :, head * HEAD_DIM:(head + 1) * HEAD_DIM] = o.astype(BF16)


def _nsa(qn, gn, kvcmp, slc, win, mmat, btail, bwin):
    B, S, _ = qn.shape
    G = NSA_GROUPS
    nq = S // NSA_Q_BLOCK
    qb = NSA_Q_BLOCK
    return pl.pallas_call(
        _nsa_kernel,
        grid=(B, nq),
        in_specs=[pl.BlockSpec((1, qb, 512), lambda b, i: (b, i, 0)),
                  pl.BlockSpec((1, qb, LANES), lambda b, i: (b, i, 0)),
                  pl.BlockSpec((1, G) + kvcmp.shape[2:], lambda b, i: (b, 0, 0, 0)),
                  pl.BlockSpec((1, G) + slc.shape[2:], lambda b, i: (b, 0, 0, 0)),
                  pl.BlockSpec((1, G) + win.shape[2:], lambda b, i: (b, 0, 0, 0)),
                  pl.BlockSpec(mmat.shape, lambda b, i: (0, 0)),
                  pl.BlockSpec(btail.shape, lambda b, i: (0, 0, 0, 0)),
                  pl.BlockSpec(bwin.shape, lambda b, i: (0, 0, 0))],
        out_specs=pl.BlockSpec((1, qb, 512), lambda b, i: (b, i, 0)),
        out_shape=jax.ShapeDtypeStruct((B, S, 512), BF16),
        compiler_params=pltpu.CompilerParams(dimension_semantics=("arbitrary", "arbitrary"),
                                             vmem_limit_bytes=VMEM_LIMIT),
    )(qn, gn, kvcmp, slc, win, mmat, btail, bwin)


def _swa_kernel(qs_ref, kv_ref, bias_ref, sink_ref, o_ref):
    step = pl.program_id(1)
    hg = SWA_HEADS // SWA_KV_HEADS
    qb = SWA_Q_BLOCK
    units = [(j, g) for j in range(SWA_BLOCKS_PER_STEP) for g in range(SWA_KV_HEADS)]

    def unit_rows(j, g):
        q = qs_ref[0, j * qb:(j + 1) * qb, :]
        qg = jnp.concatenate([q[:, (g * hg + h) * HEAD_DIM:(g * hg + h + 1) * HEAD_DIM] for h in range(hg)],
                             axis=0)
        return jnp.concatenate([qg, jnp.zeros_like(qg)], axis=1)

    def unit_keys(j, g):
        s0 = (step * SWA_BLOCKS_PER_STEP + j) * qb
        return kv_ref[0, g, pl.ds(pl.multiple_of(KV_FRONT - SWA_WINDOW + s0, qb), SWA_KEYS), :]

    def unit_bias(j, g):
        return bias_ref[jnp.minimum(step, 1), g] if j == 0 else bias_ref[1, g]

    qg2 = [unit_rows(j, g) for j, g in units]
    kv = [unit_keys(j, g) for j, g in units]
    n = range(len(units))
    s = [(_dot_nt(qg2[u], kv[u]) + unit_bias(*units[u])).astype(BF16) for u in n]
    sink = [sink_ref[g] for _, g in units]
    m = [jnp.maximum(jnp.max(s[u], axis=1, keepdims=True).astype(F32), sink[u]).astype(BF16) for u in n]
    p = [jnp.exp(s[u] - m[u]) for u in n]
    l = [jnp.sum(p[u].astype(F32), axis=1, keepdims=True) + jnp.exp(sink[u] - m[u].astype(F32)) for u in n]
    o = [_dot(p[u], kv[u])[:, HEAD_DIM:] / l[u] for u in n]
    for u, (j, g) in enumerate(units):
        for h in range(hg):
            head = g * hg + h
            o_ref[0, j * qb:(j + 1) * qb, head * HEAD_DIM:(head + 1) * HEAD_DIM] = (
                o[u][h * qb:(h + 1) * qb].astype(BF16))


def _swa(qs, kvp, bias, sink_rows):
    B, S, _ = qs.shape
    qb = SWA_Q_BLOCK * SWA_BLOCKS_PER_STEP
    return pl.pallas_call(
        _swa_kernel,
        grid=(B, S // qb),
        in_specs=[pl.BlockSpec((1, qb, 512), lambda b, i: (b, i, 0)),
                  pl.BlockSpec((1, SWA_KV_HEADS) + kvp.shape[2:], lambda b, i: (b, 0, 0, 0)),
                  pl.BlockSpec(bias.shape, lambda b, i: (0, 0, 0, 0)),
                  pl.BlockSpec(sink_rows.shape, lambda b, i: (0, 0, 0))],
        out_specs=pl.BlockSpec((1, qb, 512), lambda b, i: (b, i, 0)),
        out_shape=jax.ShapeDtypeStruct((B, S, 512), BF16),
        compiler_params=pltpu.CompilerParams(dimension_semantics=("arbitrary", "arbitrary"),
                                             vmem_limit_bytes=VMEM_LIMIT),
    )(qs, kvp, bias, sink_rows)


def _pack_bf16_pair(lo, hi):
    lo_b = pltpu.bitcast(lo.astype(BF16).astype(F32), jnp.uint32)
    hi_b = pltpu.bitcast(hi.astype(BF16).astype(F32), jnp.uint32)
    return (lo_b >> 16) | (hi_b & jnp.uint32(0xFFFF0000))


def _unpack_bf16_pair(w):
    lo = pltpu.bitcast(w << 16, F32).astype(BF16)
    hi = pltpu.bitcast(w & jnp.uint32(0xFFFF0000), F32).astype(BF16)
    return lo, hi


def _merge_kernel(on_ref, os_ref, gm_ref, x_ref, mod_ref, wb_ref, wo_ref, g_ref, wr_ref, br_ref, ltri_ref,
                  xo_ref, hp_ref, idx_ref, gate_ref, cnt_ref, run_ref):
    d = D_MODEL
    tp = ltri_ref.shape[0]
    parts = range(MERGE_PARTS)
    rows = [slice(h * tp, (h + 1) * tp) for h in parts]
    a = [_dot(on_ref[rows[h], :], wb_ref[0]) for h in parts]
    b = [_dot(os_ref[rows[h], :], wb_ref[1]) for h in parts]
    merged = [_sigmoid(gm_ref[rows[h], :d]) * a[h] + _sigmoid(gm_ref[rows[h], d:]) * b[h] for h in parts]
    y = [_dot(merged[h].astype(BF16), wo_ref[...]) for h in parts]
    xn = [x_ref[rows[h], :] + mod_ref[0, 2:3, :] * y[h] for h in parts]
    for h in parts:
        xo_ref[rows[h], :] = xn[h]
    ms = [jnp.mean(xn[h] * xn[h], axis=-1, keepdims=True) for h in parts]
    h2 = [xn[h] * lax.rsqrt(ms[h] + RMS_EPS) * g_ref[...] for h in parts]
    h2 = [h2[h] * (1.0 + mod_ref[0, 4:5, :]) + mod_ref[0, 3:4, :] for h in parts]
    for h in parts:
        hp_ref[rows[h], :] = _pack_bf16_pair(h2[h][:, :d // 2], h2[h][:, d // 2:])
    hs = [_split3(h2[h]) for h in parts]
    w_hi, w_mid, w_lo = wr_ref[0], wr_ref[1], wr_ref[2]
    logits = [(_dot(hs[h][0], w_hi) + (_dot(hs[h][0], w_mid) + _dot(hs[h][1], w_hi))
               + (_dot(hs[h][0], w_lo) + _dot(hs[h][1], w_mid) + _dot(hs[h][2], w_hi))) + br_ref[...]
              for h in parts]
    lane = lax.broadcasted_iota(jnp.int32, (tp, LANES), 1)
    c = [jnp.where(lane < N_EXPERTS, logits[h], -jnp.inf) for h in parts]
    vals, idxs = [[] for _ in parts], [[] for _ in parts]
    for _ in range(TOP_K):
        mx = [jnp.max(c[h], axis=1, keepdims=True) for h in parts]
        first = [jnp.min(jnp.where(c[h] == mx[h], lane, LANES), axis=1, keepdims=True) for h in parts]
        for h in parts:
            vals[h].append(mx[h])
            idxs[h].append(first[h])
        c = [jnp.where(lane == first[h], -jnp.inf, c[h]) for h in parts]
    es = [[jnp.exp(v - vals[h][0]) for v in vals[h]] for h in parts]
    den = [es[h][0] + es[h][1] + es[h][2] + es[h][3] for h in parts]

    @pl.when(pl.program_id(0) == 0)
    def _():
        run_ref[...] = jnp.zeros_like(run_ref)

    def expert_onehot(h):
        oh = jnp.zeros((tp, LANES), F32)
        for k in range(TOP_K):
            oh = jnp.where(lane == idxs[h][k], 1.0, oh)
        return oh

    onehot = [expert_onehot(h) for h in parts]
    prefix = [_dot(ltri_ref[...], onehot[h].astype(BF16)) for h in parts]
    seen = run_ref[...]
    before = []
    for h in parts:
        before.append(prefix[h] + seen)
        seen = seen + jnp.sum(onehot[h], axis=0, keepdims=True)
    run_ref[...] = seen
    cnt_ref[...] = seen.astype(jnp.int32)

    for h in parts:
        idx_out = jnp.zeros((tp, LANES), jnp.int32)
        gate_out = jnp.zeros((tp, LANES), F32)
        for k in range(TOP_K):
            rank = jnp.sum(jnp.where(lane == idxs[h][k], before[h], 0.0), axis=1,
                           keepdims=True).astype(jnp.int32)
            idx_out = jnp.where(lane == k, idxs[h][k], idx_out)
            idx_out = jnp.where(lane == TOP_K + k, rank, idx_out)
            gate_out = jnp.where(lane == k, es[h][k] / den[h], gate_out)
        idx_ref[rows[h], :] = idx_out
        gate_ref[rows[h], :] = gate_out


def _merge(on, os_, gm, x, mod, wb, wo, g_ffn, wr3, br, S):
    T, D = x.shape
    tp = 256
    tm = tp * MERGE_PARTS
    nb_per_seq = S // tm
    ltri = jnp.asarray(np.tril(np.ones((tp, tp), np.float32), -1), BF16)
    return pl.pallas_call(
        _merge_kernel,
        grid=(T // tm,),
        in_specs=[pl.BlockSpec((tm, 512), lambda i: (i, 0)),
                  pl.BlockSpec((tm, 512), lambda i: (i, 0)),
                  pl.BlockSpec((tm, 2 * D), lambda i: (i, 0)),
                  pl.BlockSpec((tm, D), lambda i: (i, 0)),
                  pl.BlockSpec((1, 6, D), lambda i: (i // nb_per_seq, 0, 0)),
                  pl.BlockSpec((2, 512, D), lambda i: (0, 0, 0)),
                  pl.BlockSpec((D, D), lambda i: (0, 0)),
                  pl.BlockSpec((1, D), lambda i: (0, 0)),
                  pl.BlockSpec((3, D, LANES), lambda i: (0, 0, 0)),
                  pl.BlockSpec((1, LANES), lambda i: (0, 0)),
                  pl.BlockSpec((tp, tp), lambda i: (0, 0))],
        out_specs=[pl.BlockSpec((tm, D), lambda i: (i, 0)),
                   pl.BlockSpec((tm, D // 2), lambda i: (i, 0)),
                   pl.BlockSpec((tm, LANES), lambda i: (i, 0)),
                   pl.BlockSpec((tm, LANES), lambda i: (i, 0)),
                   pl.BlockSpec((1, LANES), lambda i: (0, 0))],
        out_shape=[jax.ShapeDtypeStruct((T, D), F32),
                   jax.ShapeDtypeStruct((T, D // 2), jnp.uint32),
                   jax.ShapeDtypeStruct((T, LANES), jnp.int32),
                   jax.ShapeDtypeStruct((T, LANES), F32),
                   jax.ShapeDtypeStruct((1, LANES), jnp.int32)],
        scratch_shapes=[pltpu.VMEM((1, LANES), F32)],
        compiler_params=pltpu.CompilerParams(dimension_semantics=("arbitrary",),
                                             vmem_limit_bytes=VMEM_LIMIT),
    )(on, os_, gm, x, mod, wb, wo, g_ffn, wr3, br, ltri)


def _dispatch_kernel(dest_ref, hp_ref, xs_in_hbm, xs_hbm, sem):
    del xs_in_hbm
    tm = hp_ref.shape[0]

    def body(i, _):
        base = pl.multiple_of(i * SUBLANES, SUBLANES)
        for u in range(SUBLANES):
            for k in range(TOP_K):
                pltpu.make_async_copy(hp_ref.at[pl.ds(base + u, 1), :],
                                      xs_hbm.at[pl.ds(dest_ref[0, 0, (base + u) * TOP_K + k], 1), :],
                                      sem).start()
        return 0
    lax.fori_loop(0, tm // SUBLANES, body, 0)
    for _ in range(TOP_K):
        pltpu.make_async_copy(hp_ref, xs_hbm.at[pl.ds(0, tm), :], sem).wait()


def _dispatch(dest, hp, n_slots):
    T, W = hp.shape
    tm = 256
    xs0 = jnp.zeros((n_slots, W), hp.dtype)
    return pl.pallas_call(
        _dispatch_kernel,
        grid=(T // tm,),
        in_specs=[pl.BlockSpec((1, 1, tm * TOP_K), lambda i: (i, 0, 0), memory_space=pltpu.SMEM),
                  pl.BlockSpec((tm, W), lambda i: (i, 0)),
                  pl.BlockSpec(memory_space=pl.ANY)],
        out_specs=pl.BlockSpec(memory_space=pl.ANY),
        out_shape=jax.ShapeDtypeStruct((n_slots, W), hp.dtype),
        scratch_shapes=[pltpu.SemaphoreType.DMA(())],
        input_output_aliases={2: 0},
        compiler_params=pltpu.CompilerParams(dimension_semantics=("arbitrary",),
                                             vmem_limit_bytes=VMEM_LIMIT),
    )(dest.reshape(T // tm, 1, tm * TOP_K), hp, xs0)


def _moe_kernel(bexp_ref, used_ref, xs_ref, wu_ref, wd_ref, bu_ref, bd_ref, ys_ref):
    i = pl.program_id(0)

    @pl.when(used_ref[i] > 0)
    def _():
        x_lo, x_hi = _unpack_bf16_pair(xs_ref[...])
        half = D_MODEL // 2
        hid = []
        for t in range(2 * D_EXPERT // UP_TILE):
            cols = slice(t * UP_TILE, (t + 1) * UP_TILE)
            gu = _dot(x_lo, wu_ref[0, :half, cols]) + _dot(x_hi, wu_ref[0, half:, cols]) + bu_ref[0, :, cols]
            glu = jnp.minimum(gu[:, :UP_TILE // 2], SWIGLU_LIMIT)
            lin = jnp.clip(gu[:, UP_TILE // 2:], -SWIGLU_LIMIT, SWIGLU_LIMIT)
            hid.append((glu * _sigmoid(SWIGLU_ALPHA * glu) * (lin + 1.0)).astype(BF16))
        ys_ref[...] = _dot(jnp.concatenate(hid, axis=1), wd_ref[0]) + bd_ref[0]

    @pl.when(used_ref[i] == 0)
    def _():
        ys_ref[...] = jnp.zeros_like(ys_ref)


def _wup_prep_kernel(w_ref, p_ref, o_ref):
    for t in range(w_ref.shape[2] // UP_TILE):
        cols = slice(t * UP_TILE, (t + 1) * UP_TILE)
        o_ref[0, :, cols] = _dot(w_ref[0, :, cols].astype(BF16), p_ref[...]).astype(BF16)


def _wup_prep(w_up):
    N, D, W = w_up.shape
    j = np.arange(UP_TILE)
    perm = np.zeros((UP_TILE, UP_TILE), np.float32)
    perm[j, (j % 2) * (UP_TILE // 2) + j // 2] = 1.0
    return pl.pallas_call(
        _wup_prep_kernel,
        grid=(N,),
        in_specs=[pl.BlockSpec((1, D, W), lambda n: (n, 0, 0)),
                  pl.BlockSpec((UP_TILE, UP_TILE), lambda n: (0, 0))],
        out_specs=pl.BlockSpec((1, D, W), lambda n: (n, 0, 0)),
        out_shape=jax.ShapeDtypeStruct((N, D, W), BF16),
        compiler_params=pltpu.CompilerParams(dimension_semantics=("arbitrary",),
                                             vmem_limit_bytes=VMEM_LIMIT),
    )(w_up, jnp.asarray(perm, BF16))


def _moe(bexp, used, xs, wu, wd, bu, bd):
    nblk = bexp.shape[0]
    D = D_MODEL
    tm = MOE_BLOCK
    expert = lambda i, be, us: (be[i], 0, 0)
    grid_spec = pltpu.PrefetchScalarGridSpec(
        num_scalar_prefetch=2,
        grid=(nblk,),
        in_specs=[pl.BlockSpec((tm, D // 2), lambda i, be, us: (i, 0)),
                  pl.BlockSpec((1, D, 2 * D_EXPERT), expert),
                  pl.BlockSpec((1, D_EXPERT, D), expert),
                  pl.BlockSpec((1, 1, 2 * D_EXPERT), expert),
                  pl.BlockSpec((1, 1, D), expert)],
        out_specs=pl.BlockSpec((tm, D), lambda i, be, us: (i, 0)),
    )
    return pl.pallas_call(
        _moe_kernel,
        grid_spec=grid_spec,
        out_shape=jax.ShapeDtypeStruct((nblk * tm, D), F32),
        compiler_params=pltpu.CompilerParams(dimension_semantics=("arbitrary",),
                                             vmem_limit_bytes=VMEM_LIMIT),
    )(bexp, used, xs, wu, wd, bu, bd)


def _combine_kernel(dest_ref, destn_ref, x_ref, gate_ref, mod_ref, g_ref, ys_hbm, o_ref, ybuf, sem, *, final):
    i = pl.program_id(0)
    nt = pl.num_programs(0)
    slot = i % 2
    tm = x_ref.shape[0]

    def issue(ids_ref, sl):
        def body(i, _):
            base = pl.multiple_of(i * SUBLANES, SUBLANES)
            for u in range(SUBLANES):
                for k in range(TOP_K):
                    pltpu.make_async_copy(ys_hbm.at[pl.ds(ids_ref[0, 0, (base + u) * TOP_K + k], 1), :],
                                          ybuf.at[sl, k, pl.ds(base + u, 1), :], sem.at[sl]).start()
            return 0
        lax.fori_loop(0, tm // SUBLANES, body, 0)

    @pl.when(i == 0)
    def _():
        issue(dest_ref, 0)

    @pl.when(i + 1 < nt)
    def _():
        issue(destn_ref, 1 - slot)

    for k in range(TOP_K):
        pltpu.make_async_copy(ys_hbm.at[pl.ds(0, tm), :], ybuf.at[slot, k], sem.at[slot]).wait()

    gate = gate_ref[...]
    y = gate[:, 0:1] * ybuf[slot, 0]
    for k in range(1, TOP_K):
        y = y + gate[:, k:k + 1] * ybuf[slot, k]
    xn = x_ref[...] + mod_ref[0, 5:6, :] * y
    if final:
        ms = jnp.mean(xn * xn, axis=-1, keepdims=True)
        xn = xn * lax.rsqrt(ms + RMS_EPS) * g_ref[...]
    o_ref[...] = xn


def _combine(dest, x, ys, gate, mod, g_final, S, final):
    T, D = x.shape
    tm = 256
    nb_per_seq = S // tm
    nt = T // tm
    dest3 = dest.reshape(nt, 1, tm * TOP_K)
    smem_blk = lambda f: pl.BlockSpec((1, 1, tm * TOP_K), f, memory_space=pltpu.SMEM)
    return pl.pallas_call(
        functools.partial(_combine_kernel, final=final),
        grid=(nt,),
        in_specs=[smem_blk(lambda i: (i, 0, 0)),
                  smem_blk(lambda i: (jnp.minimum(i + 1, nt - 1), 0, 0)),
                  pl.BlockSpec((tm, D), lambda i: (i, 0)),
                  pl.BlockSpec((tm, LANES), lambda i: (i, 0)),
                  pl.BlockSpec((1, 6, D), lambda i: (i // nb_per_seq, 0, 0)),
                  pl.BlockSpec((1, D), lambda i: (0, 0)),
                  pl.BlockSpec(memory_space=pl.ANY)],
        out_specs=pl.BlockSpec((tm, D), lambda i: (i, 0)),
        out_shape=jax.ShapeDtypeStruct((T, D), F32),
        scratch_shapes=[pltpu.VMEM((2, TOP_K, tm, D), F32),
                        pltpu.SemaphoreType.DMA((2,))],
        compiler_params=pltpu.CompilerParams(dimension_semantics=("arbitrary",),
                                             vmem_limit_bytes=VMEM_LIMIT),
    )(dest3, dest3, x, gate, mod, g_final, ys)


def _t5_bucket(dist):
    n = jnp.maximum(dist, 0)
    max_exact = REL_BUCKETS // 2
    log_ratio = jnp.log(jnp.maximum(n, 1).astype(F32) / max_exact) / math.log(REL_MAX_DIST / max_exact)
    large = max_exact + (log_ratio * (REL_BUCKETS - max_exact)).astype(jnp.int32)
    return jnp.where(n < max_exact, n, jnp.minimum(large, REL_BUCKETS - 1))


def _toeplitz_bias(tab, q_block, n_keys, back, window, variants, shift_far):
    r = jnp.arange(q_block)[:, None]
    c = jnp.arange(n_keys)[None, :]
    dist = r + back - c
    ok = dist >= 0
    if window is not None:
        ok = ok & (dist < window)
    in_bucket = _t5_bucket(dist)[..., None] == jnp.arange(REL_BUCKETS)
    bias = jnp.sum(jnp.where(in_bucket[..., None, None], tab, 0.0), axis=2)
    if shift_far:
        bias = bias - tab[REL_BUCKETS - 1][None, None]
    bias = jnp.transpose(bias, (2, 3, 0, 1))
    outs = []
    for v in range(variants):
        okv = ok if v == variants - 1 else ok & (c >= back - v * q_block)
        outs.append(jnp.where(okv[None, None], bias, NEG))
    out = jnp.stack(outs)
    V, G, Hg = out.shape[:3]
    return out.reshape(V, G, Hg * q_block, n_keys).astype(F32)


def _routing_tables(route, counts, T):
    E = N_EXPERTS
    n_blocks = -(-(T * TOP_K) // MOE_BLOCK) + E
    padded = (counts + MOE_BLOCK - 1) // MOE_BLOCK * MOE_BLOCK
    pad_end = jnp.cumsum(padded)
    grp_start = pad_end - padded
    e_idx = route[:, :TOP_K]
    onehot = e_idx[..., None] == jnp.arange(E, dtype=jnp.int32)
    dest = jnp.sum(jnp.where(onehot, grp_start, 0), axis=-1) + route[:, TOP_K:2 * TOP_K]
    blk_start = jnp.arange(n_blocks, dtype=jnp.int32) * MOE_BLOCK
    bexp = jnp.minimum(jnp.sum(pad_end[None, :] <= blk_start[:, None], axis=1), E - 1).astype(jnp.int32)
    used = (blk_start < pad_end[-1]).astype(jnp.int32)
    return dest.astype(jnp.int32), bexp, used


def kernel(x, c, w_ada, b_ada, g_mix, g_ffn, g_final, w_in, cmp_pos, cmp_w1, cmp_w2, sinks, rel_tab,
           w_branch, w_out, w_router, b_router, w_up, b_up, w_down, b_down):
    B, S, D = x.shape
    L = w_ada.shape[0]
    T = B * S
    G = NSA_GROUPS
    nb = S // SEL_BLOCK
    ncp = S // CMP_STRIDE

    w_in_p = _reorder_in_proj(w_in)
    pos_r = cmp_pos.reshape(L, 2, 2, 1, CMP_STRIDE * HEAD_DIM)
    w1_r = cmp_w1.reshape(L, 2, 2, CMP_STRIDE * HEAD_DIM, CMP_HIDDEN).astype(BF16)
    w2_r = cmp_w2.astype(BF16)
    wb = w_branch.astype(BF16)
    wo = w_out.astype(BF16)
    wr_pad = jnp.pad(w_router, ((0, 0), (0, 0), (0, LANES - N_EXPERTS)))
    wr_hi = wr_pad.astype(BF16)
    wr_r1 = wr_pad - wr_hi.astype(F32)
    wr_mid = wr_r1.astype(BF16)
    wr_lo = (wr_r1 - wr_mid.astype(F32)).astype(BF16)
    wr3 = jnp.stack([wr_hi, wr_mid, wr_lo], axis=1)
    br = jnp.pad(b_router, ((0, 0), (0, LANES - N_EXPERTS))).reshape(L, 1, LANES)
    wu = _wup_prep(w_up.reshape(L * N_EXPERTS, D, 2 * D_EXPERT))
    n_up_tiles = 2 * D_EXPERT // UP_TILE
    bu = b_up.reshape(L * N_EXPERTS, n_up_tiles, UP_TILE // 2, 2).transpose(0, 1, 3, 2).reshape(
        L * N_EXPERTS, 1, 2 * D_EXPERT)
    wd = w_down.reshape(L * N_EXPERTS, D_EXPERT, D).astype(BF16)
    bd = b_down.reshape(L * N_EXPERTS, 1, D)

    nsa_tab = rel_tab[:, :NSA_HEADS].reshape(REL_BUCKETS, G, HEADS_PER_GROUP)
    swa_tab = rel_tab[:, NSA_HEADS:].reshape(REL_BUCKETS, SWA_KV_HEADS, SWA_HEADS // SWA_KV_HEADS)
    btail = _toeplitz_bias(nsa_tab, NSA_Q_BLOCK, TAIL_KEYS, SEL_NEAR, None, 3, True)
    bwin = _toeplitz_bias(nsa_tab, NSA_Q_BLOCK, WIN_KEYS, NSA_WINDOW, NSA_WINDOW, 1, False)[0]
    bswa = _toeplitz_bias(swa_tab, SWA_Q_BLOCK, SWA_KEYS, SWA_WINDOW, SWA_WINDOW, 2, False)
    ci = jnp.arange(ncp)[:, None]
    bj = jnp.arange(nb)[None, :]
    delta = (SEL_BLOCK // CMP_STRIDE) * bj - ci
    mmat = jnp.where((delta == 0) | (delta == 4), 1.0, jnp.where((delta >= 1) & (delta <= 3), 2.0, 0.0))
    mmat = jnp.where(ci < ncp - 1, mmat, 0.0).astype(BF16)
    slc_rows = KV_FRONT + S
    kpos = jnp.arange(slc_rows)[:, None] - KV_FRONT
    onehot_blk = ((kpos >= 0) & (kpos < S) & (kpos // SEL_BLOCK == jnp.arange(nb)[None, :])).astype(BF16)
    slc_t = jnp.broadcast_to(
        jnp.concatenate([jnp.zeros((slc_rows, 2 * HEAD_DIM), BF16), onehot_blk], axis=1),
        (B, G, slc_rows, 2 * HEAD_DIM + nb))
    win_t = jnp.zeros((B, G, KV_FRONT + S, 2 * HEAD_DIM), BF16)

    c_pad = jnp.pad(c, ((0, 8 - B), (0, 0)))
    mod_all = _ada_mod(c_pad, w_ada, b_ada)[:, :B].reshape(L, B, 6, D)

    xf = x
    for l in range(L):
        mod = mod_all[l]
        qn, kvc, slc, win, gn, qs, swa_kv, gm = _inproj(xf, mod, g_mix[l].reshape(1, D), w_in_p[l],
                                                        slc_t, win_t, win_t)
        xc = kvc.reshape(B, 2 * G, ncp, CMP_STRIDE * HEAD_DIM)
        kvcmp = _compress(xc, pos_r[l], w1_r[l], w2_r[l])
        o_n = _nsa(qn, gn, kvcmp, slc, win, mmat, btail, bwin)
        sink_rows = jnp.repeat(sinks[l].reshape(SWA_KV_HEADS, SWA_HEADS // SWA_KV_HEADS), SWA_Q_BLOCK,
                               axis=1)[..., None]
        o_s = _swa(qs, swa_kv, bswa, sink_rows)
        x2, hp, route, gate, counts = _merge(o_n.reshape(T, 512), o_s.reshape(T, 512), gm.reshape(T, 2 * D),
                                             xf.reshape(T, D), mod, wb[l], wo[l], g_ffn[l].reshape(1, D),
                                             wr3[l], br[l], S)
        dest, bexp, used = _routing_tables(route, counts[0, :N_EXPERTS], T)
        xs = _dispatch(dest, hp, bexp.shape[0] * MOE_BLOCK)
        ys = _moe(bexp + l * N_EXPERTS, used, xs, wu, wd, bu, bd)
        xf = _combine(dest, x2, ys, gate, mod, g_final.reshape(1, D), S,
                      final=(l == L - 1)).reshape(B, S, D)
    return xf
```
